```python
import jax, jax.numpy as jnp
from jax import lax
import numpy as np

D_MODEL = 2048
BATCH = 4
SEQ = 2048
DEPTH = 1

CTX_LEN = 256
GRID_W = 64
EPS = 1e-6

M_HEADS = 8
M_DQK = 128
M_DV = 256
M_CHUNK = 64

A_HEADS = 16
A_KV_HEADS = 4
A_GROUP = A_HEADS // A_KV_HEADS
A_DH = 128
WINDOW = 128
A_BLOCK = 128
ROPE_HALF = A_DH // 4
ROPE_BASE = 10000.0

N_GROUPS = 8
EXPERTS_PER_GROUP = 8
N_EXPERTS = N_GROUPS * EXPERTS_PER_GROUP
TOP_K = 2
D_EXPERT = 1024
MOE_BLOCK = 128

M_QK_W = M_HEADS * M_DQK
M_V_W = M_HEADS * M_DV
A_Q_W = A_HEADS * A_DH
A_KV_W = A_KV_HEADS * A_DH
IN_SIZES = (M_QK_W, M_QK_W, M_V_W, M_V_W, 4 * M_HEADS, A_Q_W, A_KV_W, A_KV_W, 2 * D_MODEL)
IN_WIDTH = sum(IN_SIZES)
IN_SPLITS = [sum(IN_SIZES[:i]) for i in range(1, len(IN_SIZES))]

kernel_name = "hybrid_mlstm_swa_hmoe_prefix_block"

F32 = jnp.float32


def rms_norm(x, g):
    xf = x.astype(F32)
    y = xf * lax.rsqrt(jnp.mean(xf * xf, axis=-1, keepdims=True) + EPS)
    return (y * g.astype(F32)).astype(x.dtype)


def adaln(cvec, w_mod, b_mod):
    return jnp.split(jax.nn.silu(cvec) @ w_mod + b_mod, 6, axis=-1)


def modulate(h, shift, scale):
    return h * (1 + scale) + shift


def to_heads(t, n_heads, d):
    b, n, _ = t.shape
    return t.reshape(b, n, n_heads, d).transpose(0, 2, 1, 3)


def from_heads(t):
    b, h, n, d = t.shape
    return t.transpose(0, 2, 1, 3).reshape(b, n, h * d)


def flip_t(t):
    return jnp.flip(t, axis=2)


def axial_rope_tables(n_tokens):
    rows = n_tokens // GRID_W
    row = jnp.repeat(jnp.arange(rows, dtype=F32), GRID_W)
    col = jnp.tile(jnp.arange(GRID_W, dtype=F32), rows)
    inv_freq = ROPE_BASE ** (-jnp.arange(ROPE_HALF, dtype=F32) / ROPE_HALF)
    ang = jnp.stack([row[:, None] * inv_freq, col[:, None] * inv_freq], axis=1)
    return jnp.cos(ang), jnp.sin(ang)


def apply_axial_rope(t, cos, sin):
    b, h, n, d = t.shape
    tt = t.astype(F32).reshape(b, h, n, 2, 2, ROPE_HALF)
    t1, t2 = tt[..., 0, :], tt[..., 1, :]
    out = jnp.stack([t1 * cos - t2 * sin, t2 * cos + t1 * sin], axis=-2)
    return out.reshape(b, h, n, d).astype(t.dtype)


def mlstm_zero_state(b):
    return (jnp.zeros((b, M_HEADS, M_DQK, M_DV), F32),
            jnp.zeros((b, M_HEADS, M_DQK), F32),
            jnp.zeros((b, M_HEADS), F32))


def _mlstm_chunk_step(carry, inp):
    C, n, m = carry
    q, k, v, ig, lf = inp
    L = q.shape[2]
    b = jnp.cumsum(lf, axis=-1)
    d = b[..., :, None] - b[..., None, :] + ig[..., None, :]
    d = jnp.where(jnp.tril(jnp.ones((L, L), bool)), d, -jnp.inf)
    inter = b + m[..., None]
    m_t = jnp.maximum(inter, jnp.max(d, axis=-1))
    w_inter = jnp.exp(inter - m_t)
    s = jnp.einsum('bhtk,bhsk->bhts', q, k) * jnp.exp(d - m_t[..., None])
    num = w_inter[..., None] * jnp.einsum('bhtk,bhkv->bhtv', q, C) + jnp.einsum('bhts,bhsv->bhtv', s, v)
    qn = w_inter * jnp.einsum('bhtk,bhk->bht', q, n) + jnp.sum(s, axis=-1)
    h = num / jnp.maximum(jnp.abs(qn), jnp.exp(-m_t))[..., None]
    b_end = b[..., -1]
    dec = b_end[..., None] - b + ig
    m_new = jnp.maximum(b_end + m, jnp.max(dec, axis=-1))
    w_s = jnp.exp(dec - m_new[..., None])
    w_c = jnp.exp(b_end + m - m_new)
    C_new = w_c[..., None, None] * C + jnp.einsum('bhs,bhsk,bhsv->bhkv', w_s, k, v)
    n_new = w_c[..., None] * n + jnp.einsum('bhs,bhsk->bhk', w_s, k)
    return (C_new, n_new, m_new), h


def mlstm_scan(q, k, v, ig, lf, state):
    b, h, t, _ = q.shape
    nc = t // M_CHUNK

    def chunks(a):
        return jnp.moveaxis(a.reshape(b, h, nc, M_CHUNK, *a.shape[3:]), 2, 0)

    state, hs = lax.scan(_mlstm_chunk_step, state, (chunks(q), chunks(k), chunks(v), chunks(ig), chunks(lf)))
    return jnp.moveaxis(hs, 0, 2).reshape(b, h, t, M_DV), state


def mlstm_inputs(q, k, v, g, gate_b):
    b, t, _ = q.shape
    q = to_heads(q, M_HEADS, M_DQK).astype(F32)
    k = to_heads(k, M_HEADS, M_DQK).astype(F32) * (M_DQK ** -0.5)
    v = to_heads(v, M_HEADS, M_DV).astype(F32)
    g = g.reshape(b, t, 4, M_HEADS).astype(F32) + gate_b.astype(F32)
    g = jnp.transpose(g, (2, 0, 3, 1))
    return q, k, v, g


def bidir_mlstm(lat, cx, need_ctx_out):
    q, k, v, g = lat
    qc, kc, vc, gc = cx
    zero = mlstm_zero_state(q.shape[0])
    lsig = jax.nn.log_sigmoid
    hf_c, st_f = mlstm_scan(qc, kc, vc, gc[0], lsig(gc[1]), zero)
    hb_c, st_b = mlstm_scan(flip_t(qc), flip_t(kc), flip_t(vc), flip_t(gc[2]), flip_t(lsig(gc[3])), zero)
    hf, _ = mlstm_scan(q, k, v, g[0], lsig(g[1]), st_f)
    hb, _ = mlstm_scan(flip_t(q), flip_t(k), flip_t(v), flip_t(g[2]), flip_t(lsig(g[3])), st_b)
    h_lat = hf + flip_t(hb)
    h_ctx = hf_c + flip_t(hb_c) if need_ctx_out else None
    return h_lat, h_ctx


def mlstm_output(hh, o, norm_g, w_br):
    hh = hh * lax.rsqrt(jnp.mean(hh * hh, axis=-1, keepdims=True) + EPS)
    hh = from_heads(hh).astype(o.dtype) * norm_g * jax.nn.sigmoid(o)
    return hh @ w_br


def windowed_gqa_with_context(q, k, v, kc, vc, sink):
    b, _, s, _ = q.shape
    nb = s // A_BLOCK
    n_ctx = kc.shape[2]
    scale = A_DH ** -0.5
    qb = q.reshape(b, A_KV_HEADS, A_GROUP, nb, A_BLOCK, A_DH)

    def band(t):
        tb = jnp.pad(t, ((0, 0), (0, 0), (A_BLOCK, A_BLOCK), (0, 0))).reshape(b, A_KV_HEADS, nb + 2, A_BLOCK, A_DH)
        return jnp.concatenate([tb[:, :, :-2], tb[:, :, 1:-1], tb[:, :, 2:]], axis=3)

    kb, vb = band(k), band(v)
    s_loc = jnp.einsum('bhgnqd,bhnkd->bhgnqk', qb, kb).astype(F32) * scale
    s_ctx = jnp.einsum('bhgnqd,bhcd->bhgnqc', qb, kc).astype(F32) * scale
    blk = jnp.arange(nb)[:, None, None]
    qpos = blk * A_BLOCK + jnp.arange(A_BLOCK)[None, :, None]
    kpos = (blk - 1) * A_BLOCK + jnp.arange(3 * A_BLOCK)[None, None, :]
    valid = (jnp.abs(qpos - kpos) <= WINDOW) & (kpos >= 0) & (kpos < s)
    s_loc = jnp.where(valid, s_loc, -jnp.inf)
    s_sink = jnp.broadcast_to(sink.astype(F32).reshape(1, A_KV_HEADS, A_GROUP, 1, 1, 1), s_loc.shape[:-1] + (1,))
    p = jax.nn.softmax(jnp.concatenate([s_loc, s_ctx, s_sink], axis=-1), axis=-1)
    p_loc = p[..., :3 * A_BLOCK].astype(v.dtype)
    p_ctx = p[..., 3 * A_BLOCK:3 * A_BLOCK + n_ctx].astype(v.dtype)
    o = jnp.einsum('bhgnqk,bhnkd->bhgnqd', p_loc, vb) + jnp.einsum('bhgnqc,bhcd->bhgnqd', p_ctx, vc)
    return o.reshape(b, A_HEADS, s, A_DH)


def context_attention(qc, kc, vc, sink):
    b, _, t, _ = qc.shape
    qg = qc.reshape(b, A_KV_HEADS, A_GROUP, t, A_DH)
    sc = jnp.einsum('bhgqd,bhcd->bhgqc', qg, kc).astype(F32) * (A_DH ** -0.5)
    s_sink = jnp.broadcast_to(sink.astype(F32).reshape(1, A_KV_HEADS, A_GROUP, 1, 1), sc.shape[:-1] + (1,))
    p = jax.nn.softmax(jnp.concatenate([sc, s_sink], axis=-1), axis=-1)[..., :-1]
    o = jnp.einsum('bhgqc,bhcd->bhgqd', p.astype(vc.dtype), vc)
    return o.reshape(b, A_HEADS, t, A_DH)


def merge_branches(br_m, br_a, gbr, w_out):
    g_m, g_a = jnp.split(jax.nn.sigmoid(gbr), 2, axis=-1)
    return (g_m * br_m + g_a * br_a) @ w_out


def token_mixers(h, hc, cos, sin, w_in, gate_b, m_norm_g, sink, w_br_m, w_br_a, w_out, update_ctx):
    qm, km, vm, om, gm, qa, ka, va, gbr = jnp.split(h @ w_in, IN_SPLITS, axis=-1)
    qmc, kmc, vmc, omc, gmc, qac, kac, vac, gbrc = jnp.split(hc @ w_in, IN_SPLITS, axis=-1)
    hm, hmc = bidir_mlstm(mlstm_inputs(qm, km, vm, gm, gate_b), mlstm_inputs(qmc, kmc, vmc, gmc, gate_b), update_ctx)
    br_m = mlstm_output(hm, om, m_norm_g, w_br_m)
    q = apply_axial_rope(to_heads(qa, A_HEADS, A_DH), cos, sin)
    k = apply_axial_rope(to_heads(ka, A_KV_HEADS, A_DH), cos, sin)
    v = to_heads(va, A_KV_HEADS, A_DH)
    kc = to_heads(kac, A_KV_HEADS, A_DH)
    vc = to_heads(vac, A_KV_HEADS, A_DH)
    br_a = from_heads(windowed_gqa_with_context(q, k, v, kc, vc, sink)) @ w_br_a
    y = merge_branches(br_m, br_a, gbr, w_out)
    yc = None
    if update_ctx:
        br_mc = mlstm_output(hmc, omc, m_norm_g, w_br_m)
        br_ac = from_heads(context_attention(to_heads(qac, A_HEADS, A_DH), kc, vc, sink)) @ w_br_a
        yc = merge_branches(br_mc, br_ac, gbrc, w_out)
    return y, yc


def hierarchical_moe(h, w_rg, b_rg, w_re, b_re, w_g, w_u, w_d):
    t_tok, d = h.shape
    grp_logits = (h @ w_rg + b_rg).astype(F32)
    p_grp = jax.nn.softmax(grp_logits, axis=-1)
    g_sel = jnp.argmax(grp_logits, axis=-1)
    exp_logits = (h @ w_re + b_re).astype(F32).reshape(t_tok, N_GROUPS, EXPERTS_PER_GROUP)
    in_grp = jnp.take_along_axis(exp_logits, g_sel[:, None, None], axis=1)[:, 0]
    top_val, top_idx = lax.top_k(in_grp, TOP_K)
    w_top = jax.nn.softmax(top_val, axis=-1) * jnp.take_along_axis(p_grp, g_sel[:, None], axis=1)
    eid = (g_sel[:, None] * EXPERTS_PER_GROUP + top_idx).reshape(-1)
    n_assign = t_tok * TOP_K
    order = jnp.argsort(eid)
    sorted_eid = eid[order]
    tok = order // TOP_K
    w_sorted = w_top.reshape(-1)[order]
    counts = jnp.bincount(eid, length=N_EXPERTS)
    padded = ((counts + MOE_BLOCK - 1) // MOE_BLOCK) * MOE_BLOCK
    start = jnp.cumsum(counts) - counts
    pend = jnp.cumsum(padded)
    pstart = pend - padded
    dest = pstart[sorted_eid] + jnp.arange(n_assign) - start[sorted_eid]
    n_blocks = -(-(n_assign + N_EXPERTS * (MOE_BLOCK - 1)) // MOE_BLOCK)
    xbuf = jnp.zeros((n_blocks * MOE_BLOCK, d), h.dtype).at[dest].set(h[tok])
    blk_expert = jnp.minimum(jnp.searchsorted(pend, jnp.arange(n_blocks) * MOE_BLOCK, side='right'), N_EXPERTS - 1)

    def run_block(args):
        xb, e = args
        return (jax.nn.silu(xb @ w_g[e]) * (xb @ w_u[e])) @ w_d[e]

    ybuf = lax.map(run_block, (xbuf.reshape(n_blocks, MOE_BLOCK, d), blk_expert)).reshape(-1, d)
    contrib = ybuf[dest] * w_sorted[:, None].astype(h.dtype)
    return jax.ops.segment_sum(contrib, tok, num_segments=t_tok)


def setup_inputs(seed: int = 0) -> dict:
    key = jax.random.key(seed)
    ks = jax.random.split(key, 24)
    nrm = jax.random.normal
    D = D_MODEL
    return {
        "x": nrm(ks[0], (BATCH, SEQ, D), F32),
        "c": nrm(ks[1], (BATCH, D), F32),
        "ctx": nrm(ks[2], (BATCH, CTX_LEN, D), F32),
        "c_ctx": nrm(ks[3], (D,), F32),
        "w_mod": nrm(ks[4], (DEPTH, D, 6 * D), F32) * D ** -0.5,
        "b_mod": 0.02 * nrm(ks[5], (DEPTH, 6 * D), F32),
        "norm1_g": 1.0 + 0.05 * nrm(ks[6], (DEPTH, D), F32),
        "w_in": nrm(ks[7], (DEPTH, D, IN_WIDTH), F32) * D ** -0.5,
        "mlstm_gate_b": jnp.array([0.0, 3.0, 0.0, 3.0], F32)[None, :, None] + 0.3 * nrm(ks[8], (DEPTH, 4, M_HEADS), F32),
        "mlstm_norm_g": 1.0 + 0.05 * nrm(ks[9], (DEPTH, M_V_W), F32),
        "attn_sink": 0.5 * nrm(ks[10], (DEPTH, A_HEADS), F32),
        "w_br_m": nrm(ks[11], (DEPTH, M_V_W, D), F32) * M_V_W ** -0.5,
        "w_br_a": nrm(ks[12], (DEPTH, A_Q_W, D), F32) * A_Q_W ** -0.5,
        "w_out": nrm(ks[13], (DEPTH, D, D), F32) * D ** -0.5,
        "norm2_g": 1.0 + 0.05 * nrm(ks[14], (DEPTH, D), F32),
        "w_router_grp": nrm(ks[15], (DEPTH, D, N_GROUPS), F32) * D ** -0.5,
        "b_router_grp": 0.01 * nrm(ks[16], (DEPTH, N_GROUPS), F32),
        "w_router_exp": nrm(ks[17], (DEPTH, D, N_EXPERTS), F32) * D ** -0.5,
        "b_router_exp": 0.01 * nrm(ks[18], (DEPTH, N_EXPERTS), F32),
        "w_exp_gate": nrm(ks[19], (DEPTH, N_EXPERTS, D, D_EXPERT), F32) * D ** -0.5,
        "w_exp_up": nrm(ks[20], (DEPTH, N_EXPERTS, D, D_EXPERT), F32) * D ** -0.5,
        "w_exp_down": nrm(ks[21], (DEPTH, N_EXPERTS, D_EXPERT, D), F32) * D_EXPERT ** -0.5,
        "final_norm_g": 1.0 + 0.05 * nrm(ks[22], (D,), F32),
    }


def reference(x, c, ctx, c_ctx, w_mod, b_mod, norm1_g, w_in, mlstm_gate_b, mlstm_norm_g, attn_sink,
              w_br_m, w_br_a, w_out, norm2_g, w_router_grp, b_router_grp, w_router_exp, b_router_exp,
              w_exp_gate, w_exp_up, w_exp_down, final_norm_g):
    b, s, d = x.shape
    n_ctx = ctx.shape[1]
    cos, sin = axial_rope_tables(s)
    xc = ctx
    for l in range(DEPTH):
        update_ctx = l < DEPTH - 1
        sh1, sc1, g1, sh2, sc2, g2 = [m[:, None, :] for m in adaln(c, w_mod[l], b_mod[l])]
        sh1c, sc1c, g1c, sh2c, sc2c, g2c = adaln(c_ctx, w_mod[l], b_mod[l])
        h = modulate(rms_norm(x, norm1_g[l]), sh1, sc1)
        hc = modulate(rms_norm(xc, norm1_g[l]), sh1c, sc1c)
        y, yc = token_mixers(h, hc, cos, sin, w_in[l], mlstm_gate_b[l], mlstm_norm_g[l], attn_sink[l],
                             w_br_m[l], w_br_a[l], w_out[l], update_ctx)
        x = x + g1 * y
        h = modulate(rms_norm(x, norm2_g[l]), sh2, sc2)
        moe_args = (w_router_grp[l], b_router_grp[l], w_router_exp[l], b_router_exp[l],
                    w_exp_gate[l], w_exp_up[l], w_exp_down[l])
        if update_ctx:
            xc = xc + g1c * yc
            hc = modulate(rms_norm(xc, norm2_g[l]), sh2c, sc2c)
            out = hierarchical_moe(jnp.concatenate([hc.reshape(-1, d), h.reshape(-1, d)], axis=0), *moe_args)
            xc = xc + g2c * out[:b * n_ctx].reshape(b, n_ctx, d)
            x = x + g2 * out[b * n_ctx:].reshape(b, s, d)
        else:
            x = x + g2 * hierarchical_moe(h.reshape(-1, d), *moe_args).reshape(b, s, d)
    return rms_norm(x, final_norm_g)
```

```python
import functools

import jax
import jax.numpy as jnp
from jax import lax
from jax.experimental import pallas as pl
from jax.experimental.pallas import tpu as pltpu

F32 = jnp.float32
BF16 = jnp.bfloat16

D_MODEL = 2048
EPS = 1e-6
GRID_W = 64

M_HEADS = 8
M_DQK = 128
M_DV = 256
M_CHUNK = 256

A_HEADS = 16
A_KV_HEADS = 4
A_GROUP = A_HEADS // A_KV_HEADS
A_DH = 128
A_BLOCK = 128
WINDOW = 128
ROPE_HALF = A_DH // 4
ROPE_BASE = 10000.0

N_GROUPS = 8
EXPERTS_PER_GROUP = 8
N_EXPERTS = N_GROUPS * EXPERTS_PER_GROUP
TOP_K = 2
D_EXPERT = 1024

M_QK_W = M_HEADS * M_DQK
M_V_W = M_HEADS * M_DV
A_Q_W = A_HEADS * A_DH
A_KV_W = A_KV_HEADS * A_DH
GATE_W = 4 * M_HEADS

OFF_QM = 0
OFF_KM = OFF_QM + M_QK_W
OFF_VM = OFF_KM + M_QK_W
OFF_OM = OFF_VM + M_V_W
OFF_GM = OFF_OM + M_V_W
OFF_QA = OFF_GM + GATE_W
OFF_KA = OFF_QA + A_Q_W
OFF_VA = OFF_KA + A_KV_W
OFF_GBR = OFF_VA + A_KV_W
IN_WIDTH = OFF_GBR + 2 * D_MODEL

P_QM = 0
P_KM = P_QM + M_QK_W
P_VM = P_KM + M_QK_W
P_OM = P_VM + M_V_W
P_QA = P_OM + M_V_W
P_KA = P_QA + A_Q_W
P_VA = P_KA + A_KV_W
P_GBM = P_VA + A_KV_W
P_GBA = P_GBM + D_MODEL
P_WIDTH = P_GBA + D_MODEL
C_KM = 0
C_VM = C_KM + M_QK_W
C_KA = C_VM + M_V_W
C_VA = C_KA + A_KV_W
C_WIDTH = C_VA + A_KV_W

LANES = 128
MOE_SUB = 128
MOE_SUPER = 512
MOE_FC = 512
VMEM_LIMIT = 56 * 1024 * 1024


def _cparams(sem, vmem=VMEM_LIMIT):
    return pltpu.CompilerParams(dimension_semantics=sem, vmem_limit_bytes=vmem)


def _adaln_kernel(c_ref, w_ref, b_ref, o_ref):
    c = c_ref[...]
    a = (c * jax.nn.sigmoid(c)).astype(BF16)
    o_ref[...] = jnp.dot(a, w_ref[...].astype(BF16), preferred_element_type=F32) + b_ref[...]


def _adaln(cvec, w_mod, b_mod):
    rows, d = cvec.shape
    n = w_mod.shape[1]
    tn = 1024
    return pl.pallas_call(
        _adaln_kernel,
        grid=(n // tn,),
        in_specs=[pl.BlockSpec((rows, d), lambda j: (0, 0)),
                  pl.BlockSpec((d, tn), lambda j: (0, j)),
                  pl.BlockSpec((1, tn), lambda j: (0, j))],
        out_specs=pl.BlockSpec((rows, tn), lambda j: (0, j)),
        out_shape=jax.ShapeDtypeStruct((rows, n), F32),
        compiler_params=_cparams(("parallel",)),
        name="adaln",
    )(cvec, w_mod, b_mod.reshape(1, n))


def _norm_proj_kernel(x_ref, g_ref, sh_ref, sc_ref, w_ref, o_ref, h_ref):
    @pl.when(pl.program_id(2) == 0)
    def _():
        x = x_ref[0]
        y = x * lax.rsqrt(jnp.mean(x * x, axis=-1, keepdims=True) + EPS)
        y = y * g_ref[...]
        h_ref[...] = (y * (1.0 + sc_ref[0]) + sh_ref[0]).astype(BF16)

    o_ref[0] = jnp.dot(h_ref[...], w_ref[...], preferred_element_type=F32).astype(o_ref.dtype)


def _norm_proj(x, g, shift, scale, w, out_dtype, tm, tn):
    b, s, d = x.shape
    n = w.shape[1]
    return pl.pallas_call(
        _norm_proj_kernel,
        grid=(b, s // tm, n // tn),
        in_specs=[pl.BlockSpec((1, tm, d), lambda bi, i, j: (bi, i, 0)),
                  pl.BlockSpec((1, d), lambda bi, i, j: (0, 0)),
                  pl.BlockSpec((1, 1, d), lambda bi, i, j: (bi, 0, 0)),
                  pl.BlockSpec((1, 1, d), lambda bi, i, j: (bi, 0, 0)),
                  pl.BlockSpec((d, tn), lambda bi, i, j: (0, j))],
        out_specs=pl.BlockSpec((1, tm, tn), lambda bi, i, j: (bi, i, j)),
        out_shape=jax.ShapeDtypeStruct((b, s, n), out_dtype),
        scratch_shapes=[pltpu.VMEM((tm, d), BF16)],
        compiler_params=_cparams(("parallel", "parallel", "arbitrary")),
        name="norm_proj",
    )(x, g.reshape(1, d), shift, scale, w)


def _log_sigmoid(x):
    return jnp.minimum(x, 0.0) - jnp.log(1.0 + jnp.exp(-jnp.abs(x)))


def _mlstm_chunk(direction, qv, kv, vv, gcol, grow, c_ref, n_ref, m_ref, lower, upper):
    gi = 2 * direction
    igc = gcol[:, gi:gi + 1]
    lfc = _log_sigmoid(gcol[:, gi + 1:gi + 2])
    igr = grow[gi:gi + 1, :]
    lfr = _log_sigmoid(grow[gi + 1:gi + 2, :])
    valid, other = (lower, upper) if direction == 0 else (upper, lower)
    kscale = M_DQK ** -0.5

    b_c = jnp.sum(jnp.where(valid, lfr, 0.0), axis=1, keepdims=True)
    b_end = jnp.sum(lfc, axis=0, keepdims=True)
    c_st = c_ref[direction]
    n_st = n_ref[direction]
    m_st = m_ref[direction][:, 0:1]

    h = None
    if qv is not None:
        b_r = jnp.sum(jnp.where(other, lfc, 0.0), axis=0, keepdims=True)
        dm = jnp.where(valid, b_c - b_r + igr, -jnp.inf)
        inter = b_c + m_st
        m_t = jnp.maximum(inter, jnp.max(dm, axis=1, keepdims=True))
        w_inter = jnp.exp(inter - m_t)
        sc = lax.dot_general(qv, kv, (((1,), (1,)), ((), ())), preferred_element_type=F32) * kscale
        s = sc * jnp.exp(dm - m_t)
        num = (w_inter * jnp.dot(qv, c_st.astype(BF16), preferred_element_type=F32)
               + jnp.dot(s.astype(BF16), vv, preferred_element_type=F32))
        qn = (w_inter * jnp.sum(qv.astype(F32) * n_st, axis=1, keepdims=True)
              + jnp.sum(s, axis=1, keepdims=True))
        h = num / jnp.maximum(jnp.abs(qn), jnp.exp(-m_t))

    dec = b_end - b_c + igc
    m_new = jnp.maximum(b_end + m_st, jnp.max(dec, axis=0, keepdims=True))
    w_s = jnp.exp(dec - m_new) * kscale
    w_c = jnp.exp(b_end + m_st - m_new)
    wv = (w_s * vv.astype(F32)).astype(BF16)
    c_ref[direction] = w_c * c_st + lax.dot_general(kv, wv, (((0,), (0,)), ((), ())),
                                                    preferred_element_type=F32)
    n_ref[direction] = w_c * n_st + jnp.sum(w_s * kv.astype(F32), axis=0, keepdims=True)
    m_ref[direction] = jnp.broadcast_to(m_new, (1, LANES))
    return h


def _mlstm_kernel(q_ref, k_ref, v_ref, o_ref, kc_ref, vc_ref, gcol_ref, grow_ref, ng_ref, out_ref,
                  c_ref, n_ref, m_ref, hf_ref, hb_ref, *, n_ctx_chunks, n_lat_chunks):
    L = M_CHUNK
    row = lax.broadcasted_iota(jnp.int32, (L, L), 0)
    col = lax.broadcasted_iota(jnp.int32, (L, L), 1)
    lower = col <= row
    upper = col >= row

    c_ref[...] = jnp.zeros_like(c_ref)
    n_ref[...] = jnp.zeros_like(n_ref)
    m_ref[...] = jnp.zeros_like(m_ref)

    for j in range(n_ctx_chunks):
        for direction, jj in ((0, j), (1, n_ctx_chunks - 1 - j)):
            _mlstm_chunk(direction, None, kc_ref[0, jj * L:(jj + 1) * L, :], vc_ref[0, jj * L:(jj + 1) * L, :],
                         gcol_ref[0, 0, jj], grow_ref[0, 0, jj], c_ref, n_ref, m_ref, lower, upper)

    def body(j, carry):
        for direction, jj, h_ref in ((0, j, hf_ref), (1, n_lat_chunks - 1 - j, hb_ref)):
            r0 = pl.multiple_of(jj * L, L)
            h = _mlstm_chunk(direction, q_ref[0, pl.ds(r0, L), :], k_ref[0, pl.ds(r0, L), :],
                             v_ref[0, pl.ds(r0, L), :], gcol_ref[0, 0, n_ctx_chunks + jj],
                             grow_ref[0, 0, n_ctx_chunks + jj], c_ref, n_ref, m_ref, lower, upper)
            h_ref[pl.ds(r0, L), :] = h
        return carry

    lax.fori_loop(0, n_lat_chunks, body, 0)

    def finish(j, carry):
        r0 = pl.multiple_of(j * L, L)
        hh = hf_ref[pl.ds(r0, L), :] + hb_ref[pl.ds(r0, L), :]
        hh = hh * lax.rsqrt(jnp.mean(hh * hh, axis=-1, keepdims=True) + EPS)
        og = jax.nn.sigmoid(o_ref[0, pl.ds(r0, L), :].astype(F32))
        out_ref[0, pl.ds(r0, L), :] = (hh * ng_ref[...] * og).astype(out_ref.dtype)
        return carry

    lax.fori_loop(0, n_lat_chunks, finish, 0)


def _mlstm(p_lat, p_ctx, gcol, grow, norm_g):
    b, s, _ = p_lat.shape
    n_ctx = p_ctx.shape[1]
    L = M_CHUNK
    nck, nlk = n_ctx // L, s // L
    kern = functools.partial(_mlstm_kernel, n_ctx_chunks=nck, n_lat_chunks=nlk)
    return pl.pallas_call(
        kern,
        grid=(b, M_HEADS),
        in_specs=[pl.BlockSpec((1, s, M_DQK), lambda bi, h: (bi, 0, P_QM // M_DQK + h)),
                  pl.BlockSpec((1, s, M_DQK), lambda bi, h: (bi, 0, P_KM // M_DQK + h)),
                  pl.BlockSpec((1, s, M_DV), lambda bi, h: (bi, 0, P_VM // M_DV + h)),
                  pl.BlockSpec((1, s, M_DV), lambda bi, h: (bi, 0, P_OM // M_DV + h)),
                  pl.BlockSpec((1, n_ctx, M_DQK), lambda bi, h: (bi, 0, C_KM // M_DQK + h)),
                  pl.BlockSpec((1, n_ctx, M_DV), lambda bi, h: (bi, 0, C_VM // M_DV + h)),
                  pl.BlockSpec((1, 1, nck + nlk, L, 4), lambda bi, h: (bi, h, 0, 0, 0)),
                  pl.BlockSpec((1, 1, nck + nlk, 4, L), lambda bi, h: (bi, h, 0, 0, 0)),
                  pl.BlockSpec((1, M_DV), lambda bi, h: (0, h))],
        out_specs=pl.BlockSpec((1, s, M_DV), lambda bi, h: (bi, 0, h)),
        out_shape=jax.ShapeDtypeStruct((b, s, M_V_W), BF16),
        scratch_shapes=[pltpu.VMEM((2, M_DQK, M_DV), F32),
                        pltpu.VMEM((2, 1, M_DQK), F32),
                        pltpu.VMEM((2, 1, LANES), F32),
                        pltpu.VMEM((s, M_DV), F32),
                        pltpu.VMEM((s, M_DV), F32)],
        compiler_params=_cparams(("parallel", "parallel")),
        name="mlstm",
    )(p_lat, p_lat, p_lat, p_lat, p_ctx, p_ctx, gcol, grow, norm_g.reshape(1, M_V_W))


def _rope(t, cos_t, sin_s, lane_lo):
    swapped = jnp.where(lane_lo, pltpu.roll(t, A_DH - ROPE_HALF, 1), pltpu.roll(t, ROPE_HALF, 1))
    return t * cos_t + swapped * sin_s


def _attn_kernel(sink_ref, q_ref, k_ref, v_ref, kc_ref, vc_ref, cos_ref, sin_ref, out_ref, kpad, vpad):
    s = k_ref.shape[1]
    blk = A_BLOCK
    hk = pl.program_id(1)
    scale = A_DH ** -0.5
    lane = lax.broadcasted_iota(jnp.int32, (1, A_DH), 1)
    lane_lo = (lane % (2 * ROPE_HALF)) < ROPE_HALF

    zero = jnp.zeros((blk, A_DH), BF16)
    kpad[0:blk, :] = zero
    kpad[s + blk:s + 2 * blk, :] = zero
    vpad[0:blk, :] = zero
    vpad[s + blk:s + 2 * blk, :] = zero
    vpad[blk:s + blk, :] = v_ref[0]

    def rope_k(n, carry):
        r0 = pl.multiple_of(n * blk, blk)
        kf = k_ref[0, pl.ds(r0, blk), :].astype(F32)
        kr = _rope(kf, cos_ref[pl.ds(r0, blk), :], sin_ref[pl.ds(r0, blk), :], lane_lo)
        kpad[pl.ds(r0 + blk, blk), :] = kr.astype(BF16)
        return carry

    lax.fori_loop(0, s // blk, rope_k, 0)

    kc = kc_ref[0]
    vc = vc_ref[0]
    ii = lax.broadcasted_iota(jnp.int32, (blk, 3 * blk), 0)
    jj = lax.broadcasted_iota(jnp.int32, (blk, 3 * blk), 1)
    rel = jj - ii
    band = (rel >= blk - WINDOW) & (rel <= blk + WINDOW)
    nt = (((1,), (1,)), ((), ()))

    def body(n, carry):
        r0 = pl.multiple_of(n * blk, blk)
        cos_q = cos_ref[pl.ds(r0, blk), :]
        sin_q = sin_ref[pl.ds(r0, blk), :]
        kb = kpad[pl.ds(r0, 3 * blk), :]
        vb = vpad[pl.ds(r0, 3 * blk), :]
        kpos = (n - 1) * blk + jj
        valid = band & (kpos >= 0) & (kpos < s)
        for g in range(A_GROUP):
            qf = q_ref[0, pl.ds(r0, blk), g * A_DH:(g + 1) * A_DH].astype(F32)
            qg = _rope(qf, cos_q, sin_q, lane_lo).astype(BF16)
            s_loc = lax.dot_general(qg, kb, nt, preferred_element_type=F32) * scale
            s_loc = jnp.where(valid, s_loc, -jnp.inf)
            s_ctx = lax.dot_general(qg, kc, nt, preferred_element_type=F32) * scale
            snk = sink_ref[hk * A_GROUP + g]
            m = jnp.maximum(jnp.maximum(jnp.max(s_loc, axis=1, keepdims=True),
                                        jnp.max(s_ctx, axis=1, keepdims=True)), snk)
            p_loc = jnp.exp(s_loc - m)
            p_ctx = jnp.exp(s_ctx - m)
            den = (jnp.sum(p_loc, axis=1, keepdims=True) + jnp.sum(p_ctx, axis=1, keepdims=True)
                   + jnp.exp(snk - m))
            o = (jnp.dot(p_loc.astype(BF16), vb, preferred_element_type=F32)
                 + jnp.dot(p_ctx.astype(BF16), vc, preferred_element_type=F32)) / den
            out_ref[0, pl.ds(r0, blk), g * A_DH:(g + 1) * A_DH] = o.astype(out_ref.dtype)
        return carry

    lax.fori_loop(0, s // blk, body, 0)


def _attention(p_lat, p_ctx, sink, cos_t, sin_s):
    b, s, _ = p_lat.shape
    n_ctx = p_ctx.shape[1]
    gw = A_GROUP * A_DH
    return pl.pallas_call(
        _attn_kernel,
        grid=(b, A_KV_HEADS),
        in_specs=[pl.BlockSpec(memory_space=pltpu.SMEM),
                  pl.BlockSpec((1, s, gw), lambda bi, h: (bi, 0, P_QA // gw + h)),
                  pl.BlockSpec((1, s, A_DH), lambda bi, h: (bi, 0, P_KA // A_DH + h)),
                  pl.BlockSpec((1, s, A_DH), lambda bi, h: (bi, 0, P_VA // A_DH + h)),
                  pl.BlockSpec((1, n_ctx, A_DH), lambda bi, h: (bi, 0, C_KA // A_DH + h)),
                  pl.BlockSpec((1, n_ctx, A_DH), lambda bi, h: (bi, 0, C_VA // A_DH + h)),
                  pl.BlockSpec((s, A_DH), lambda bi, h: (0, 0)),
                  pl.BlockSpec((s, A_DH), lambda bi, h: (0, 0))],
        out_specs=pl.BlockSpec((1, s, gw), lambda bi, h: (bi, 0, h)),
        out_shape=jax.ShapeDtypeStruct((b, s, A_Q_W), BF16),
        scratch_shapes=[pltpu.VMEM((s + 2 * A_BLOCK, A_DH), BF16),
                        pltpu.VMEM((s + 2 * A_BLOCK, A_DH), BF16)],
        compiler_params=_cparams(("parallel", "parallel")),
        name="attention",
    )(sink, p_lat, p_lat, p_lat, p_ctx, p_ctx, cos_t, sin_s)


def _rope_tables(n_tokens):
    rows = n_tokens // GRID_W
    row = jnp.repeat(jnp.arange(rows, dtype=F32), GRID_W)
    col = jnp.tile(jnp.arange(GRID_W, dtype=F32), rows)
    inv_freq = ROPE_BASE ** (-jnp.arange(ROPE_HALF, dtype=F32) / ROPE_HALF)
    ar = row[:, None] * inv_freq
    ac = col[:, None] * inv_freq
    cos_t = jnp.concatenate([jnp.cos(ar), jnp.cos(ar), jnp.cos(ac), jnp.cos(ac)], axis=1)
    sin_s = jnp.concatenate([-jnp.sin(ar), jnp.sin(ar), -jnp.sin(ac), jnp.sin(ac)], axis=1)
    return cos_t, sin_s


def _merge_kernel(hm_ref, at_ref, wm_ref, wa_ref, gm_ref, ga_ref, o_ref):
    bm = jnp.dot(hm_ref[...], wm_ref[...], preferred_element_type=F32)
    ba = jnp.dot(at_ref[...], wa_ref[...], preferred_element_type=F32)
    o = jax.nn.sigmoid(gm_ref[...].astype(F32)) * bm + jax.nn.sigmoid(ga_ref[...].astype(F32)) * ba
    o_ref[...] = o.astype(o_ref.dtype)


def _merge(hm, att, p2d, w_br_m, w_br_a):
    t, d = hm.shape
    tm, tn = 1024, 512
    return pl.pallas_call(
        _merge_kernel,
        grid=(t // tm, d // tn),
        in_specs=[pl.BlockSpec((tm, M_V_W), lambda i, j: (i, 0)),
                  pl.BlockSpec((tm, A_Q_W), lambda i, j: (i, 0)),
                  pl.BlockSpec((M_V_W, tn), lambda i, j: (0, j)),
                  pl.BlockSpec((A_Q_W, tn), lambda i, j: (0, j)),
                  pl.BlockSpec((tm, tn), lambda i, j: (i, P_GBM // tn + j)),
                  pl.BlockSpec((tm, tn), lambda i, j: (i, P_GBA // tn + j))],
        out_specs=pl.BlockSpec((tm, tn), lambda i, j: (i, j)),
        out_shape=jax.ShapeDtypeStruct((t, d), BF16),
        compiler_params=_cparams(("parallel", "arbitrary")),
        name="merge",
    )(hm, att, w_br_m, w_br_a, p2d, p2d)


def _outproj_kernel(mg_ref, x_ref, wo_ref, g1_ref, ng_ref, sh_ref, sc_ref, wr_ref, br_ref,
                    x1_ref, h2_ref, ri_ref, rw_ref):
    y = jnp.dot(mg_ref[0], wo_ref[...], preferred_element_type=F32)
    x1 = x_ref[0] + g1_ref[0] * y
    x1_ref[0] = x1
    hn = x1 * lax.rsqrt(jnp.mean(x1 * x1, axis=-1, keepdims=True) + EPS) * ng_ref[...]
    h2 = (hn * (1.0 + sc_ref[0]) + sh_ref[0]).astype(BF16)
    h2_ref[0] = h2

    logits = jnp.dot(h2, wr_ref[...], preferred_element_type=F32) + br_ref[...]
    lane = lax.broadcasted_iota(jnp.int32, logits.shape, 1)
    neg = -jnp.inf
    gl = jnp.where(lane < N_GROUPS, logits, neg)
    gmax = jnp.max(gl, axis=1, keepdims=True)
    gsel = jnp.min(jnp.where(gl == gmax, lane, LANES), axis=1, keepdims=True)
    p_sel = 1.0 / jnp.sum(jnp.exp(gl - gmax), axis=1, keepdims=True)
    lo = N_GROUPS + gsel * EXPERTS_PER_GROUP
    el = jnp.where((lane >= lo) & (lane < lo + EXPERTS_PER_GROUP), logits, neg)
    v1 = jnp.max(el, axis=1, keepdims=True)
    i1 = jnp.min(jnp.where(el == v1, lane, LANES), axis=1, keepdims=True)
    el2 = jnp.where(lane == i1, neg, el)
    v2 = jnp.max(el2, axis=1, keepdims=True)
    i2 = jnp.min(jnp.where(el2 == v2, lane, LANES), axis=1, keepdims=True)
    e2 = jnp.exp(v2 - v1)
    w1 = p_sel / (1.0 + e2)
    w2 = p_sel * e2 / (1.0 + e2)
    ri_ref[0] = jnp.where(lane == 0, i1 - N_GROUPS, jnp.where(lane == 1, i2 - N_GROUPS, 0))
    rw_ref[0] = jnp.where(lane == 0, w1, jnp.where(lane == 1, w2, 0.0))


def _outproj(merged, x, w_out, g1, norm_g, shift, scale, w_r, b_r):
    b, s, d = x.shape
    tm = 512
    row = lambda bi, i: (bi, i, 0)
    per_b = lambda bi, i: (bi, 0, 0)
    fixed = lambda bi, i: (0, 0)
    return pl.pallas_call(
        _outproj_kernel,
        grid=(b, s // tm),
        in_specs=[pl.BlockSpec((1, tm, d), row),
                  pl.BlockSpec((1, tm, d), row),
                  pl.BlockSpec((d, d), fixed),
                  pl.BlockSpec((1, 1, d), per_b),
                  pl.BlockSpec((1, d), fixed),
                  pl.BlockSpec((1, 1, d), per_b),
                  pl.BlockSpec((1, 1, d), per_b),
                  pl.BlockSpec((d, LANES), fixed),
                  pl.BlockSpec((1, LANES), fixed)],
        out_specs=[pl.BlockSpec((1, tm, d), row),
                   pl.BlockSpec((1, tm, d), row),
                   pl.BlockSpec((1, tm, LANES), row),
                   pl.BlockSpec((1, tm, LANES), row)],
        out_shape=[jax.ShapeDtypeStruct((b, s, d), F32),
                   jax.ShapeDtypeStruct((b, s, d), BF16),
                   jax.ShapeDtypeStruct((b, s, LANES), jnp.int32),
                   jax.ShapeDtypeStruct((b, s, LANES), F32)],
        compiler_params=_cparams(("parallel", "parallel")),
        name="outproj_router",
    )(merged, x, w_out, g1, norm_g.reshape(1, d), shift, scale, w_r, b_r)


def _moe_kernel(sbe_ref, nsub_ref, nused_ref, x_ref, wg_ref, wu_ref, wd_ref, o_ref,
                acc_ref, wgb, wub, wdb, *, n_f):
    s = pl.program_id(0)
    f = pl.program_id(1)

    @pl.when(s < nused_ref[0])
    def _():
        wgb[...] = wg_ref[...].astype(BF16)
        wub[...] = wu_ref[...].astype(BF16)
        wdb[...] = wd_ref[...].astype(BF16)

        @pl.when(f == 0)
        def _():
            acc_ref[...] = jnp.zeros_like(acc_ref)

        def body(i, carry):
            r0 = pl.multiple_of(i * MOE_SUB, MOE_SUB)
            xb = x_ref[pl.ds(r0, MOE_SUB), :]
            hg = jnp.dot(xb, wgb[...], preferred_element_type=F32)
            hu = jnp.dot(xb, wub[...], preferred_element_type=F32)
            a = (hg * jax.nn.sigmoid(hg) * hu).astype(BF16)
            acc_ref[pl.ds(r0, MOE_SUB), :] += jnp.dot(a, wdb[...], preferred_element_type=F32)
            return carry

        lax.fori_loop(0, nsub_ref[s], body, 0)

        @pl.when(f == n_f - 1)
        def _():
            o_ref[...] = acc_ref[...].astype(o_ref.dtype)


def _moe_experts(xbuf, sb_expert, sb_nsub, n_used, w_g, w_u, w_d):
    rows, d = xbuf.shape
    n_super = rows // MOE_SUPER
    n_f = D_EXPERT // MOE_FC

    def s_eff(s, nused):
        return jnp.minimum(s, nused[0] - 1)

    def f_eff(s, f, nused):
        return jnp.where(s < nused[0], f, n_f - 1)

    x_map = lambda s, f, sbe, nsub, nused: (s_eff(s, nused), 0)
    gu_map = lambda s, f, sbe, nsub, nused: (0, sbe[s_eff(s, nused)], 0, f_eff(s, f, nused))
    d_map = lambda s, f, sbe, nsub, nused: (0, sbe[s_eff(s, nused)], f_eff(s, f, nused), 0)
    grid_spec = pltpu.PrefetchScalarGridSpec(
        num_scalar_prefetch=3,
        grid=(n_super, n_f),
        in_specs=[pl.BlockSpec((MOE_SUPER, d), x_map),
                  pl.BlockSpec((None, None, d, MOE_FC), gu_map),
                  pl.BlockSpec((None, None, d, MOE_FC), gu_map),
                  pl.BlockSpec((None, None, MOE_FC, d), d_map)],
        out_specs=pl.BlockSpec((MOE_SUPER, d), x_map),
        scratch_shapes=[pltpu.VMEM((MOE_SUPER, d), F32),
                        pltpu.VMEM((d, MOE_FC), BF16),
                        pltpu.VMEM((d, MOE_FC), BF16),
                        pltpu.VMEM((MOE_FC, d), BF16)],
    )
    return pl.pallas_call(
        functools.partial(_moe_kernel, n_f=n_f),
        grid_spec=grid_spec,
        out_shape=jax.ShapeDtypeStruct((rows, d), BF16),
        compiler_params=_cparams(("arbitrary", "arbitrary")),
        name="moe_experts",
    )(sb_expert, sb_nsub, n_used, xbuf, w_g, w_u, w_d)


def _dispatch_tables(eid, n_super):
    flat = eid.reshape(-1)
    n_assign = flat.shape[0]
    order = jnp.argsort(flat)
    sorted_eid = flat[order]
    counts = jnp.bincount(flat, length=N_EXPERTS)
    nsup = (counts + MOE_SUPER - 1) // MOE_SUPER
    sup_end = jnp.cumsum(nsup)
    sup_start = sup_end - nsup
    start = jnp.cumsum(counts) - counts
    pos = jnp.arange(n_assign) - start[sorted_eid]
    dest_sorted = (sup_start[sorted_eid] * MOE_SUPER + pos).astype(jnp.int32)
    src_tok = jnp.zeros((n_super * MOE_SUPER,), jnp.int32).at[dest_sorted].set((order // TOP_K).astype(jnp.int32))
    dest = jnp.zeros((n_assign,), jnp.int32).at[order].set(dest_sorted)
    sidx = jnp.arange(n_super)
    sb_expert = jnp.minimum(jnp.searchsorted(sup_end, sidx, side='right'), N_EXPERTS - 1)
    n_used = sup_end[-1]
    rows_in = jnp.clip(counts[sb_expert] - (sidx - sup_start[sb_expert]) * MOE_SUPER, 0, MOE_SUPER)
    sb_nsub = jnp.where(sidx < n_used, (rows_in + MOE_SUB - 1) // MOE_SUB, 0)
    return (src_tok, dest.reshape(-1, TOP_K), sb_expert.astype(jnp.int32), sb_nsub.astype(jnp.int32),
            n_used.astype(jnp.int32).reshape(1))


def _final_kernel(x1_ref, ya_ref, yb_ref, w_ref, g2_ref, fg_ref, o_ref):
    w = w_ref[0]
    moe = w[:, 0:1] * ya_ref[0].astype(F32) + w[:, 1:2] * yb_ref[0].astype(F32)
    x2 = x1_ref[0] + g2_ref[0] * moe
    o_ref[0] = x2 * lax.rsqrt(jnp.mean(x2 * x2, axis=-1, keepdims=True) + EPS) * fg_ref[...]


def _final(x1, ya, yb, w_top, g2, final_g):
    b, s, d = x1.shape
    ts = 512
    row = lambda bi, i: (bi, i, 0)
    return pl.pallas_call(
        _final_kernel,
        grid=(b, s // ts),
        in_specs=[pl.BlockSpec((1, ts, d), row),
                  pl.BlockSpec((1, ts, d), row),
                  pl.BlockSpec((1, ts, d), row),
                  pl.BlockSpec((1, ts, TOP_K), row),
                  pl.BlockSpec((1, 1, d), lambda bi, i: (bi, 0, 0)),
                  pl.BlockSpec((1, d), lambda bi, i: (0, 0))],
        out_specs=pl.BlockSpec((1, ts, d), row),
        out_shape=jax.ShapeDtypeStruct((b, s, d), F32),
        compiler_params=_cparams(("parallel", "parallel")),
        name="final_norm",
    )(x1, ya, yb, w_top, g2, final_g.reshape(1, d))


def kernel(x, c, ctx, c_ctx, w_mod, b_mod, norm1_g, w_in, mlstm_gate_b, mlstm_norm_g, attn_sink, w_br_m, w_br_a, w_out, norm2_g, w_router_grp, b_router_grp, w_router_exp, b_router_exp, w_exp_gate, w_exp_up, w_exp_down, final_norm_g):
    b, s, d = x.shape
    n_ctx = ctx.shape[1]
    assert w_mod.shape[0] == 1, "single-layer block"

    pad_rows = 8 - (b + 1)
    cvec = jnp.concatenate([c, c_ctx[None, :], jnp.zeros((pad_rows, d), F32)], axis=0)
    mods = _adaln(cvec, w_mod[0], b_mod[0])
    sh1, sc1, g1, sh2, sc2, g2 = [mods[:b, i * d:(i + 1) * d].reshape(b, 1, d) for i in range(6)]
    sh1c = jnp.broadcast_to(mods[b, 0:d].reshape(1, 1, d), (b, 1, d))
    sc1c = jnp.broadcast_to(mods[b, d:2 * d].reshape(1, 1, d), (b, 1, d))

    w = w_in[0]
    w_lat = jnp.concatenate([w[:, :OFF_GM], w[:, OFF_QA:]], axis=1).astype(BF16)
    w_ctx = jnp.concatenate([w[:, OFF_KM:OFF_OM], w[:, OFF_KA:OFF_GBR]], axis=1).astype(BF16)
    w_gate = jnp.pad(w[:, OFF_GM:OFF_QA], ((0, 0), (0, LANES - GATE_W))).astype(BF16)

    p_lat = _norm_proj(x, norm1_g[0], sh1, sc1, w_lat, BF16, 1024, 512)
    g_lat = _norm_proj(x, norm1_g[0], sh1, sc1, w_gate, F32, 1024, LANES)
    p_ctx = _norm_proj(ctx, norm1_g[0], sh1c, sc1c, w_ctx, BF16, n_ctx, 512)
    g_ctx = _norm_proj(ctx, norm1_g[0], sh1c, sc1c, w_gate, F32, n_ctx, LANES)

    t_all = n_ctx + s
    gates = jnp.concatenate([g_ctx[:, :, :GATE_W], g_lat[:, :, :GATE_W]], axis=1)
    gates = gates.reshape(b, t_all, 4, M_HEADS) + mlstm_gate_b[0].astype(F32)
    gates = gates.reshape(b, t_all // M_CHUNK, M_CHUNK, 4, M_HEADS)
    gcol = jnp.transpose(gates, (0, 4, 1, 2, 3))
    grow = jnp.transpose(gates, (0, 4, 1, 3, 2))

    hm = _mlstm(p_lat, p_ctx, gcol, grow, mlstm_norm_g[0])
    cos_t, sin_s = _rope_tables(s)
    att = _attention(p_lat, p_ctx, attn_sink[0], cos_t, sin_s)

    merged = _merge(hm.reshape(b * s, M_V_W), att.reshape(b * s, A_Q_W), p_lat.reshape(b * s, P_WIDTH),
                    w_br_m[0].astype(BF16), w_br_a[0].astype(BF16))

    w_r = jnp.concatenate([w_router_grp[0], w_router_exp[0]], axis=1)
    n_r = w_r.shape[1]
    w_r = jnp.pad(w_r, ((0, 0), (0, LANES - n_r))).astype(BF16)
    b_r = jnp.pad(jnp.concatenate([b_router_grp[0], b_router_exp[0]]), (0, LANES - n_r)).reshape(1, LANES)
    x1, h2, r_idx, r_w = _outproj(merged.reshape(b, s, d), x, w_out[0].astype(BF16), g1, norm2_g[0],
                                  sh2, sc2, w_r, b_r)

    t_tok = b * s
    eid = r_idx.reshape(t_tok, LANES)[:, :TOP_K]
    w_top = r_w[:, :, :TOP_K]
    n_super = (t_tok * TOP_K) // MOE_SUPER + N_EXPERTS
    src_tok, dest, sb_expert, sb_nsub, n_used = _dispatch_tables(eid, n_super)
    xbuf = jnp.take(h2.reshape(t_tok, d), src_tok, axis=0)
    ybuf = _moe_experts(xbuf, sb_expert, sb_nsub, n_used, w_exp_gate, w_exp_up, w_exp_down)
    ya = jnp.take(ybuf, dest[:, 0], axis=0).reshape(b, s, d)
    yb = jnp.take(ybuf, dest[:, 1], axis=0).reshape(b, s, d)

    return _final(x1, ya, yb, w_top, g2, final_norm_g)
```

```python
import functools

import jax
import jax.numpy as jnp
from jax import lax
from jax.experimental import pallas as pl
from jax.experimental.pallas import tpu as pltpu

F32 = jnp.float32
BF16 = jnp.bfloat16

D_MODEL = 2048
EPS = 1e-6
GRID_W = 64

M_HEADS = 8
M_DQK = 128
M_DV = 256
M_CHUNK = 256
assert M_DQK == 128, "mLSTM state rows are laid out one key dim per lane"

A_HEADS = 16
A_KV_HEADS = 4
A_GROUP = A_HEADS // A_KV_HEADS
A_DH = 128
A_BLOCK = 128
WINDOW = 128
ROPE_HALF = A_DH // 4
ROPE_BASE = 10000.0
assert WINDOW == A_BLOCK, "band masks assume the window equals the block size"

N_GROUPS = 8
EXPERTS_PER_GROUP = 8
N_EXPERTS = N_GROUPS * EXPERTS_PER_GROUP
TOP_K = 2
D_EXPERT = 1024

M_QK_W = M_HEADS * M_DQK
M_V_W = M_HEADS * M_DV
A_Q_W = A_HEADS * A_DH
A_KV_W = A_KV_HEADS * A_DH
GATE_W = 4 * M_HEADS

OFF_QM = 0
OFF_KM = OFF_QM + M_QK_W
OFF_VM = OFF_KM + M_QK_W
OFF_OM = OFF_VM + M_V_W
OFF_GM = OFF_OM + M_V_W
OFF_QA = OFF_GM + GATE_W
OFF_KA = OFF_QA + A_Q_W
OFF_VA = OFF_KA + A_KV_W
OFF_GBR = OFF_VA + A_KV_W
IN_WIDTH = OFF_GBR + 2 * D_MODEL

P_QM = 0
P_KM = P_QM + M_QK_W
P_VM = P_KM + M_QK_W
P_OM = P_VM + M_V_W
P_QA = P_OM + M_V_W
P_KA = P_QA + A_Q_W
P_VA = P_KA + A_KV_W
P_GBM = P_VA + A_KV_W
P_GBA = P_GBM + D_MODEL
P_WIDTH = P_GBA + D_MODEL
C_KM = 0
C_VM = C_KM + M_QK_W
C_KA = C_VM + M_V_W
C_VA = C_KA + A_KV_W
C_WIDTH = C_VA + A_KV_W

LANES = 128
MOE_SUB = 128
MOE_SUPER = 512
MOE_FC = 512
GATHER_UNROLL = 8
VMEM_LIMIT = 56 * 1024 * 1024


def _cparams(sem, vmem=VMEM_LIMIT):
    return pltpu.CompilerParams(dimension_semantics=sem, vmem_limit_bytes=vmem)


def _adaln_kernel(c_ref, w_ref, b_ref, o_ref):
    c = c_ref[...]
    a = (c * jax.nn.sigmoid(c)).astype(BF16)
    o_ref[...] = jnp.dot(a, w_ref[...].astype(BF16), preferred_element_type=F32) + b_ref[...]


def _adaln(cvec, w_mod, b_mod):
    rows, d = cvec.shape
    n = w_mod.shape[1]
    tn = 1024
    return pl.pallas_call(
        _adaln_kernel,
        grid=(n // tn,),
        in_specs=[pl.BlockSpec((rows, d), lambda j: (0, 0)),
                  pl.BlockSpec((d, tn), lambda j: (0, j)),
                  pl.BlockSpec((1, tn), lambda j: (0, j))],
        out_specs=pl.BlockSpec((rows, tn), lambda j: (0, j)),
        out_shape=jax.ShapeDtypeStruct((rows, n), F32),
        compiler_params=_cparams(("parallel",)),
        name="adaln",
    )(cvec, w_mod, b_mod.reshape(1, n))


def _norm_proj_kernel(x_ref, g_ref, sh_ref, sc_ref, w_ref, o_ref, h_ref):
    @pl.when(pl.program_id(2) == 0)
    def _():
        x = x_ref[0]
        y = x * lax.rsqrt(jnp.mean(x * x, axis=-1, keepdims=True) + EPS)
        y = y * g_ref[...]
        h_ref[...] = (y * (1.0 + sc_ref[0]) + sh_ref[0]).astype(BF16)

    o_ref[0] = jnp.dot(h_ref[...], w_ref[...], preferred_element_type=F32).astype(o_ref.dtype)


def _norm_proj(x, g, shift, scale, w, out_dtype, tm, tn):
    b, s, d = x.shape
    n = w.shape[1]
    return pl.pallas_call(
        _norm_proj_kernel,
        grid=(b, s // tm, n // tn),
        in_specs=[pl.BlockSpec((1, tm, d), lambda bi, i, j: (bi, i, 0)),
                  pl.BlockSpec((1, d), lambda bi, i, j: (0, 0)),
                  pl.BlockSpec((1, 1, d), lambda bi, i, j: (bi, 0, 0)),
                  pl.BlockSpec((1, 1, d), lambda bi, i, j: (bi, 0, 0)),
                  pl.BlockSpec((d, tn), lambda bi, i, j: (0, j))],
        out_specs=pl.BlockSpec((1, tm, tn), lambda bi, i, j: (bi, i, j)),
        out_shape=jax.ShapeDtypeStruct((b, s, n), out_dtype),
        scratch_shapes=[pltpu.VMEM((tm, d), BF16)],
        compiler_params=_cparams(("parallel", "parallel", "arbitrary")),
        name="norm_proj",
    )(x, g.reshape(1, d), shift, scale, w)


def _log_sigmoid(x):
    return jnp.minimum(x, 0.0) - jnp.log(1.0 + jnp.exp(-jnp.abs(x)))


def _gate_proj_kernel(x_ref, g_ref, sh_ref, sc_ref, w_ref, b_ref, o_ref):
    x = x_ref[0]
    y = x * lax.rsqrt(jnp.mean(x * x, axis=-1, keepdims=True) + EPS) * g_ref[...]
    h = (y * (1.0 + sc_ref[0]) + sh_ref[0]).astype(BF16)
    g = jnp.dot(h, w_ref[...], preferred_element_type=F32) + b_ref[...]
    lf = _log_sigmoid(g)
    L = g.shape[0]
    row = lax.broadcasted_iota(jnp.int32, (L, L), 0)
    col = lax.broadcasted_iota(jnp.int32, (L, L), 1)
    hi = lax.Precision.HIGHEST
    cum_f = jnp.dot((col <= row).astype(F32), lf, precision=hi, preferred_element_type=F32)
    cum_b = jnp.dot((col >= row).astype(F32), lf, precision=hi, preferred_element_type=F32)
    lane = lax.broadcasted_iota(jnp.int32, g.shape, 1)
    is_ff = (lane >= M_HEADS) & (lane < 2 * M_HEADS)
    is_fb = (lane >= 3 * M_HEADS) & (lane < 4 * M_HEADS)
    o_ref[0] = jnp.where(is_ff, cum_f, jnp.where(is_fb, cum_b, g))


def _gate_proj(x, g, shift, scale, w_gate, b_gate):
    b, s, d = x.shape
    L = M_CHUNK
    return pl.pallas_call(
        _gate_proj_kernel,
        grid=(b, s // L),
        in_specs=[pl.BlockSpec((1, L, d), lambda bi, i: (bi, i, 0)),
                  pl.BlockSpec((1, d), lambda bi, i: (0, 0)),
                  pl.BlockSpec((1, 1, d), lambda bi, i: (bi, 0, 0)),
                  pl.BlockSpec((1, 1, d), lambda bi, i: (bi, 0, 0)),
                  pl.BlockSpec((d, LANES), lambda bi, i: (0, 0)),
                  pl.BlockSpec((1, LANES), lambda bi, i: (0, 0))],
        out_specs=pl.BlockSpec((1, L, LANES), lambda bi, i: (bi, i, 0)),
        out_shape=jax.ShapeDtypeStruct((b, s, LANES), F32),
        compiler_params=_cparams(("parallel", "parallel")),
        name="gate_proj",
    )(x, g.reshape(1, d), shift, scale, w_gate, b_gate)


def _mlstm_chunk(direction, qv, kv, ktv, vv, gcol, grow, c_ref, n_ref, m_ref, valid):
    gi = 2 * direction
    igr = grow[gi:gi + 1, :]
    b_r = grow[gi + 1:gi + 2, :]
    L = b_r.shape[1]
    igc = jnp.broadcast_to(gcol[:, gi:gi + 1], (L, LANES))
    b_c = jnp.broadcast_to(gcol[:, gi + 1:gi + 2], (L, LANES))
    wide = lambda a, n: jnp.concatenate([a] * (n // LANES), axis=1)
    b_end = b_r[:, L - 1:L] if direction == 0 else b_r[:, 0:1]
    kscale = M_DQK ** -0.5

    c_st = c_ref[direction]
    n_st = n_ref[direction]
    m_st = m_ref[direction]

    h = None
    if qv is not None:
        dm = jnp.where(valid, wide(b_c, L) - b_r + igr, -jnp.inf)
        inter = b_c + m_st
        m_t = jnp.maximum(inter, jnp.max(dm, axis=1, keepdims=True))
        w_inter = jnp.exp(inter - m_t)
        sc = lax.dot_general(qv, kv, (((1,), (1,)), ((), ())), preferred_element_type=F32) * kscale
        s = sc * jnp.exp(dm - wide(m_t, L))
        num = (wide(w_inter, M_DV) * jnp.dot(qv, c_st.astype(BF16), preferred_element_type=F32)
               + jnp.dot(s.astype(BF16), vv, preferred_element_type=F32))
        qn = (w_inter * jnp.sum(qv.astype(F32) * n_st, axis=1, keepdims=True)
              + jnp.sum(s, axis=1, keepdims=True))
        h = num * wide(1.0 / jnp.maximum(jnp.abs(qn), jnp.exp(-m_t)), M_DV)

    dec = b_end - b_c + igc
    m_new = jnp.maximum(b_end + m_st, jnp.max(dec, axis=0, keepdims=True))
    w_s = jnp.exp(dec - m_new) * kscale
    w_c = jnp.exp(b_end + m_st - m_new)
    wv = (wide(w_s, M_DV) * vv.astype(F32)).astype(BF16)
    c_ref[direction] = wide(w_c, M_DV) * c_st + jnp.dot(ktv, wv, preferred_element_type=F32)
    n_ref[direction] = w_c * n_st + jnp.sum(w_s * kv.astype(F32), axis=0, keepdims=True)
    m_ref[direction] = m_new
    return h


def _mlstm_kernel(q_ref, k_ref, kt_ref, v_ref, o_ref, kc_ref, ktc_ref, vc_ref, gcol_ref, grow_ref, ng_ref,
                  out_ref, c_ref, n_ref, m_ref, hf_ref, hb_ref, *, n_ctx_chunks, n_lat_chunks):
    L = M_CHUNK
    row = lax.broadcasted_iota(jnp.int32, (L, L), 0)
    col = lax.broadcasted_iota(jnp.int32, (L, L), 1)
    valid = (col <= row, col >= row)

    c_ref[...] = jnp.zeros_like(c_ref)
    n_ref[...] = jnp.zeros_like(n_ref)
    m_ref[...] = jnp.zeros_like(m_ref)

    for j in range(n_ctx_chunks):
        for direction, jj in ((0, j), (1, n_ctx_chunks - 1 - j)):
            _mlstm_chunk(direction, None, kc_ref[0, jj * L:(jj + 1) * L, :], ktc_ref[0, 0, jj],
                         vc_ref[0, jj * L:(jj + 1) * L, :], gcol_ref[0, 0, jj], grow_ref[0, 0, jj],
                         c_ref, n_ref, m_ref, valid[direction])

    def body(j, carry):
        for direction, jj, h_ref in ((0, j, hf_ref), (1, n_lat_chunks - 1 - j, hb_ref)):
            r0 = pl.multiple_of(jj * L, L)
            h = _mlstm_chunk(direction, q_ref[0, pl.ds(r0, L), :], k_ref[0, pl.ds(r0, L), :],
                             kt_ref[0, 0, jj], v_ref[0, pl.ds(r0, L), :],
                             gcol_ref[0, 0, n_ctx_chunks + jj], grow_ref[0, 0, n_ctx_chunks + jj],
                             c_ref, n_ref, m_ref, valid[direction])
            h_ref[pl.ds(r0, L), :] = h
        return carry

    lax.fori_loop(0, n_lat_chunks, body, 0)

    def finish(j, carry):
        r0 = pl.multiple_of(j * L, L)
        hh = hf_ref[pl.ds(r0, L), :] + hb_ref[pl.ds(r0, L), :]
        hh = hh * lax.rsqrt(jnp.mean(hh * hh, axis=-1, keepdims=True) + EPS)
        og = jax.nn.sigmoid(o_ref[0, pl.ds(r0, L), :].astype(F32))
        out_ref[0, pl.ds(r0, L), :] = (hh * ng_ref[...] * og).astype(out_ref.dtype)
        return carry

    lax.fori_loop(0, n_lat_chunks, finish, 0)


def _chunked_kt(p, off, n_chunks):
    b = p.shape[0]
    k = p[:, :, off:off + M_QK_W].reshape(b, n_chunks, M_CHUNK, M_HEADS, M_DQK)
    return jnp.transpose(k, (0, 3, 1, 4, 2))


def _mlstm(p_lat, p_ctx, gcol, grow, norm_g):
    b, s, _ = p_lat.shape
    n_ctx = p_ctx.shape[1]
    L = M_CHUNK
    nck, nlk = n_ctx // L, s // L
    kt_lat = _chunked_kt(p_lat, P_KM, nlk)
    kt_ctx = _chunked_kt(p_ctx, C_KM, nck)
    kern = functools.partial(_mlstm_kernel, n_ctx_chunks=nck, n_lat_chunks=nlk)
    return pl.pallas_call(
        kern,
        grid=(b, M_HEADS),
        in_specs=[pl.BlockSpec((1, s, M_DQK), lambda bi, h: (bi, 0, P_QM // M_DQK + h)),
                  pl.BlockSpec((1, s, M_DQK), lambda bi, h: (bi, 0, P_KM // M_DQK + h)),
                  pl.BlockSpec((1, 1, nlk, M_DQK, L), lambda bi, h: (bi, h, 0, 0, 0)),
                  pl.BlockSpec((1, s, M_DV), lambda bi, h: (bi, 0, P_VM // M_DV + h)),
                  pl.BlockSpec((1, s, M_DV), lambda bi, h: (bi, 0, P_OM // M_DV + h)),
                  pl.BlockSpec((1, n_ctx, M_DQK), lambda bi, h: (bi, 0, C_KM // M_DQK + h)),
                  pl.BlockSpec((1, 1, nck, M_DQK, L), lambda bi, h: (bi, h, 0, 0, 0)),
                  pl.BlockSpec((1, n_ctx, M_DV), lambda bi, h: (bi, 0, C_VM // M_DV + h)),
                  pl.BlockSpec((1, 1, nck + nlk, L, 4), lambda bi, h: (bi, h, 0, 0, 0)),
                  pl.BlockSpec((1, 1, nck + nlk, 4, L), lambda bi, h: (bi, h, 0, 0, 0)),
                  pl.BlockSpec((1, M_DV), lambda bi, h: (0, h))],
        out_specs=pl.BlockSpec((1, s, M_DV), lambda bi, h: (bi, 0, h)),
        out_shape=jax.ShapeDtypeStruct((b, s, M_V_W), BF16),
        scratch_shapes=[pltpu.VMEM((2, M_DQK, M_DV), F32),
                        pltpu.VMEM((2, 1, M_DQK), F32),
                        pltpu.VMEM((2, 1, LANES), F32),
                        pltpu.VMEM((s, M_DV), F32),
                        pltpu.VMEM((s, M_DV), F32)],
        compiler_params=_cparams(("parallel", "parallel")),
        name="mlstm",
    )(p_lat, p_lat, kt_lat, p_lat, p_lat, p_ctx, kt_ctx, p_ctx, gcol, grow, norm_g.reshape(1, M_V_W))


def _rope(t, cos_t, sin_s, lane_lo):
    swapped = jnp.where(lane_lo, pltpu.roll(t, A_DH - ROPE_HALF, 1), pltpu.roll(t, ROPE_HALF, 1))
    return t * cos_t + swapped * sin_s


def _attn_kernel(sink_ref, q_ref, k_ref, v_ref, kc_ref, vc_ref, cos_ref, sin_ref, out_ref, kpad, vpad):
    s = k_ref.shape[1]
    blk = A_BLOCK
    hk = pl.program_id(1)
    scale = A_DH ** -0.5
    lane = lax.broadcasted_iota(jnp.int32, (1, A_DH), 1)
    lane_lo = (lane % (2 * ROPE_HALF)) < ROPE_HALF

    zero = jnp.zeros((blk, A_DH), BF16)
    kpad[0:blk, :] = zero
    kpad[s + blk:s + 2 * blk, :] = zero
    vpad[0:blk, :] = zero
    vpad[s + blk:s + 2 * blk, :] = zero
    vpad[blk:s + blk, :] = v_ref[0]

    def rope_k(n, carry):
        r0 = pl.multiple_of(n * blk, blk)
        kf = k_ref[0, pl.ds(r0, blk), :].astype(F32)
        kr = _rope(kf, cos_ref[pl.ds(r0, blk), :], sin_ref[pl.ds(r0, blk), :], lane_lo)
        kpad[pl.ds(r0 + blk, blk), :] = kr.astype(BF16)
        return carry

    lax.fori_loop(0, s // blk, rope_k, 0)

    kc = kc_ref[0]
    vc = vc_ref[0]
    n_ctx = kc.shape[0]
    n_blocks = s // blk
    rr = lax.broadcasted_iota(jnp.int32, (blk, blk), 0)
    cc = lax.broadcasted_iota(jnp.int32, (blk, blk), 1)
    nt = (((1,), (1,)), ((), ()))
    log2e = 1.4426950408889634
    neg = -jnp.inf

    def body(n, carry):
        r0 = pl.multiple_of(n * blk, blk)
        cos_q = cos_ref[pl.ds(r0, blk), :]
        sin_q = sin_ref[pl.ds(r0, blk), :]
        kb = kpad[pl.ds(r0, 3 * blk), :]
        vb = vpad[pl.ds(r0, 3 * blk), :]
        bias_prev = jnp.where(cc >= rr, jnp.where(n > 0, 0.0, neg), neg)
        bias_next = jnp.where(cc <= rr, jnp.where(n < n_blocks - 1, 0.0, neg), neg)
        q_all = jnp.concatenate(
            [_rope(q_ref[0, pl.ds(r0, blk), g * A_DH:(g + 1) * A_DH].astype(F32), cos_q, sin_q,
                   lane_lo).astype(BF16) for g in range(A_GROUP)], axis=0)
        s_loc = lax.dot_general(q_all, kb, nt, preferred_element_type=F32)
        s_ctx = lax.dot_general(q_all, kc, nt, preferred_element_type=F32)
        p_parts, inv_den = [], []
        for g in range(A_GROUP):
            rows = slice(g * blk, (g + 1) * blk)
            tiles = ([s_loc[rows, 0:blk] + bias_prev, s_loc[rows, blk:2 * blk],
                      s_loc[rows, 2 * blk:3 * blk] + bias_next]
                     + [s_ctx[rows, c:c + blk] for c in range(0, n_ctx, blk)])
            tmax = functools.reduce(jnp.maximum, tiles)
            snk = sink_ref[hk * A_GROUP + g]
            m = jnp.maximum(jnp.max(tmax, axis=1, keepdims=True) * scale, snk)
            mb = m * log2e
            p = [jnp.exp2(t * (scale * log2e) - mb) for t in tiles]
            den = jnp.sum(functools.reduce(jnp.add, p), axis=1, keepdims=True) + jnp.exp(snk - m)
            inv_den.append(1.0 / den)
            p_parts.append(jnp.concatenate([t.astype(BF16) for t in p], axis=1))
        p_all = jnp.concatenate(p_parts, axis=0)
        o_all = (jnp.dot(p_all[:, :3 * blk], vb, preferred_element_type=F32)
                 + jnp.dot(p_all[:, 3 * blk:], vc, preferred_element_type=F32))
        for g in range(A_GROUP):
            o = o_all[g * blk:(g + 1) * blk, :] * inv_den[g]
            out_ref[0, pl.ds(r0, blk), g * A_DH:(g + 1) * A_DH] = o.astype(out_ref.dtype)
        return carry

    lax.fori_loop(0, n_blocks, body, 0)


def _attention(p_lat, p_ctx, sink, cos_t, sin_s):
    b, s, _ = p_lat.shape
    n_ctx = p_ctx.shape[1]
    gw = A_GROUP * A_DH
    return pl.pallas_call(
        _attn_kernel,
        grid=(b, A_KV_HEADS),
        in_specs=[pl.BlockSpec(memory_space=pltpu.SMEM),
                  pl.BlockSpec((1, s, gw), lambda bi, h: (bi, 0, P_QA // gw + h)),
                  pl.BlockSpec((1, s, A_DH), lambda bi, h: (bi, 0, P_KA // A_DH + h)),
                  pl.BlockSpec((1, s, A_DH), lambda bi, h: (bi, 0, P_VA // A_DH + h)),
                  pl.BlockSpec((1, n_ctx, A_DH), lambda bi, h: (bi, 0, C_KA // A_DH + h)),
                  pl.BlockSpec((1, n_ctx, A_DH), lambda bi, h: (bi, 0, C_VA // A_DH + h)),
                  pl.BlockSpec((s, A_DH), lambda bi, h: (0, 0)),
                  pl.BlockSpec((s, A_DH), lambda bi, h: (0, 0))],
        out_specs=pl.BlockSpec((1, s, gw), lambda bi, h: (bi, 0, h)),
        out_shape=jax.ShapeDtypeStruct((b, s, A_Q_W), BF16),
        scratch_shapes=[pltpu.VMEM((s + 2 * A_BLOCK, A_DH), BF16),
                        pltpu.VMEM((s + 2 * A_BLOCK, A_DH), BF16)],
        compiler_params=_cparams(("parallel", "parallel")),
        name="attention",
    )(sink, p_lat, p_lat, p_lat, p_ctx, p_ctx, cos_t, sin_s)


def _rope_tables(n_tokens):
    rows = n_tokens // GRID_W
    row = jnp.repeat(jnp.arange(rows, dtype=F32), GRID_W)
    col = jnp.tile(jnp.arange(GRID_W, dtype=F32), rows)
    inv_freq = ROPE_BASE ** (-jnp.arange(ROPE_HALF, dtype=F32) / ROPE_HALF)
    ar = row[:, None] * inv_freq
    ac = col[:, None] * inv_freq
    cos_t = jnp.concatenate([jnp.cos(ar), jnp.cos(ar), jnp.cos(ac), jnp.cos(ac)], axis=1)
    sin_s = jnp.concatenate([-jnp.sin(ar), jnp.sin(ar), -jnp.sin(ac), jnp.sin(ac)], axis=1)
    return cos_t, sin_s


def _merge_kernel(hm_ref, at_ref, wm_ref, wa_ref, gm_ref, ga_ref, o_ref):
    bm = jnp.dot(hm_ref[...], wm_ref[...], preferred_element_type=F32)
    ba = jnp.dot(at_ref[...], wa_ref[...], preferred_element_type=F32)
    o = jax.nn.sigmoid(gm_ref[...].astype(F32)) * bm + jax.nn.sigmoid(ga_ref[...].astype(F32)) * ba
    o_ref[...] = o.astype(o_ref.dtype)


def _merge(hm, att, p2d, w_br_m, w_br_a):
    t, d = hm.shape
    tm, tn = 1024, 512
    return pl.pallas_call(
        _merge_kernel,
        grid=(t // tm, d // tn),
        in_specs=[pl.BlockSpec((tm, M_V_W), lambda i, j: (i, 0)),
                  pl.BlockSpec((tm, A_Q_W), lambda i, j: (i, 0)),
                  pl.BlockSpec((M_V_W, tn), lambda i, j: (0, j)),
                  pl.BlockSpec((A_Q_W, tn), lambda i, j: (0, j)),
                  pl.BlockSpec((tm, tn), lambda i, j: (i, P_GBM // tn + j)),
                  pl.BlockSpec((tm, tn), lambda i, j: (i, P_GBA // tn + j))],
        out_specs=pl.BlockSpec((tm, tn), lambda i, j: (i, j)),
        out_shape=jax.ShapeDtypeStruct((t, d), BF16),
        compiler_params=_cparams(("parallel", "arbitrary")),
        name="merge",
    )(hm, att, w_br_m, w_br_a, p2d, p2d)


def _outproj_kernel(mg_ref, x_ref, wo_ref, g1_ref, ng_ref, sh_ref, sc_ref, wr_ref, br_ref,
                    x1_ref, h2_ref, ri_ref, rw_ref):
    y = jnp.dot(mg_ref[0], wo_ref[...], preferred_element_type=F32)
    x1 = x_ref[0] + g1_ref[0] * y
    x1_ref[0] = x1
    hn = x1 * lax.rsqrt(jnp.mean(x1 * x1, axis=-1, keepdims=True) + EPS) * ng_ref[...]
    h2 = hn * (1.0 + sc_ref[0]) + sh_ref[0]
    h2_ref[0] = h2

    logits = jnp.dot(h2.astype(BF16), wr_ref[...], preferred_element_type=F32) + br_ref[...]
    lane = lax.broadcasted_iota(jnp.int32, logits.shape, 1)
    neg = -jnp.inf
    gl = jnp.where(lane < N_GROUPS, logits, neg)
    gmax = jnp.max(gl, axis=1, keepdims=True)
    gsel = jnp.min(jnp.where(gl == gmax, lane, LANES), axis=1, keepdims=True)
    p_sel = 1.0 / jnp.sum(jnp.exp(gl - gmax), axis=1, keepdims=True)
    lo = N_GROUPS + gsel * EXPERTS_PER_GROUP
    el = jnp.where((lane >= lo) & (lane < lo + EXPERTS_PER_GROUP), logits, neg)
    v1 = jnp.max(el, axis=1, keepdims=True)
    i1 = jnp.min(jnp.where(el == v1, lane, LANES), axis=1, keepdims=True)
    el2 = jnp.where(lane == i1, neg, el)
    v2 = jnp.max(el2, axis=1, keepdims=True)
    i2 = jnp.min(jnp.where(el2 == v2, lane, LANES), axis=1, keepdims=True)
    e2 = jnp.exp(v2 - v1)
    w1 = p_sel / (1.0 + e2)
    w2 = p_sel * e2 / (1.0 + e2)
    ri_ref[0] = jnp.where(lane == 0, i1 - N_GROUPS, jnp.where(lane == 1, i2 - N_GROUPS, 0))
    rw_ref[0] = jnp.where(lane == 0, w1, jnp.where(lane == 1, w2, 0.0))


def _outproj(merged, x, w_out, g1, norm_g, shift, scale, w_r, b_r):
    b, s, d = x.shape
    tm = 512
    row = lambda bi, i: (bi, i, 0)
    per_b = lambda bi, i: (bi, 0, 0)
    fixed = lambda bi, i: (0, 0)
    return pl.pallas_call(
        _outproj_kernel,
        grid=(b, s // tm),
        in_specs=[pl.BlockSpec((1, tm, d), row),
                  pl.BlockSpec((1, tm, d), row),
                  pl.BlockSpec((d, d), fixed),
                  pl.BlockSpec((1, 1, d), per_b),
                  pl.BlockSpec((1, d), fixed),
                  pl.BlockSpec((1, 1, d), per_b),
                  pl.BlockSpec((1, 1, d), per_b),
                  pl.BlockSpec((d, LANES), fixed),
                  pl.BlockSpec((1, LANES), fixed)],
        out_specs=[pl.BlockSpec((1, tm, d), row),
                   pl.BlockSpec((1, tm, d), row),
                   pl.BlockSpec((1, tm, LANES), row),
                   pl.BlockSpec((1, tm, LANES), row)],
        out_shape=[jax.ShapeDtypeStruct((b, s, d), F32),
                   jax.ShapeDtypeStruct((b, s, d), F32),
                   jax.ShapeDtypeStruct((b, s, LANES), jnp.int32),
                   jax.ShapeDtypeStruct((b, s, LANES), F32)],
        compiler_params=_cparams(("parallel", "parallel")),
        name="outproj_router",
    )(merged, x, w_out, g1, norm_g.reshape(1, d), shift, scale, w_r, b_r)


def _moe_kernel(sbe_ref, nsub_ref, base_ref, nused_ref, tok_ref, h_hbm, wg_ref, wu_ref, wd_ref, o_ref,
                xbuf, sem, acc_ref, wgb, wub, wdb, *, n_f):
    s = pl.program_id(0)
    f = pl.program_id(1)
    nused = nused_ref[0]
    n_assign = tok_ref.shape[0]

    def start_gather(sb, slot):
        base = base_ref[sb]

        def issue(i, carry):
            for u in range(GATHER_UNROLL):
                r = i * GATHER_UNROLL + u
                tok = tok_ref[jnp.minimum(base + r, n_assign - 1)]
                pltpu.make_async_copy(h_hbm.at[pl.ds(tok, 1)], xbuf.at[slot, pl.ds(r, 1)],
                                      sem.at[slot]).start()
            return carry

        lax.fori_loop(0, nsub_ref[sb] * (MOE_SUB // GATHER_UNROLL), issue, 0)

    def wait_gather(sb, slot):
        def wait_sub(i, carry):
            pltpu.make_async_copy(h_hbm.at[pl.ds(0, MOE_SUB)], xbuf.at[slot, pl.ds(0, MOE_SUB)],
                                  sem.at[slot]).wait()
            return carry

        lax.fori_loop(0, nsub_ref[sb], wait_sub, 0)

    @pl.when((s >= nused) & (f == 0))
    def _():
        o_ref[...] = jnp.zeros_like(o_ref)

    @pl.when(s < nused)
    def _():
        slot = s % 2

        @pl.when(f == 0)
        def _():
            @pl.when(s == 0)
            def _():
                start_gather(0, 0)

            wait_gather(s, slot)

            @pl.when(s + 1 < nused)
            def _():
                start_gather(s + 1, 1 - slot)

            acc_ref[...] = jnp.zeros_like(acc_ref)

        wgb[...] = wg_ref[...].astype(BF16)
        wub[...] = wu_ref[...].astype(BF16)
        wdb[...] = wd_ref[...].astype(BF16)

        def body(i, carry):
            r0 = pl.multiple_of(i * MOE_SUB, MOE_SUB)
            xb = xbuf[slot, pl.ds(r0, MOE_SUB), :].astype(BF16)
            hg = jnp.dot(xb, wgb[...], preferred_element_type=F32)
            hu = jnp.dot(xb, wub[...], preferred_element_type=F32)
            a = (hg * jax.nn.sigmoid(hg) * hu).astype(BF16)
            acc_ref[pl.ds(r0, MOE_SUB), :] += jnp.dot(a, wdb[...], preferred_element_type=F32)
            return carry

        lax.fori_loop(0, nsub_ref[s], body, 0)

        @pl.when(f == n_f - 1)
        def _():
            o_ref[...] = acc_ref[...].astype(o_ref.dtype)


def _moe_experts(h2, sorted_tok, sb_expert, sb_nsub, sb_base, n_used, n_super, w_g, w_u, w_d):
    _, d = h2.shape
    n_f = D_EXPERT // MOE_FC

    def s_eff(s, nused):
        return jnp.minimum(s, nused[0] - 1)

    def f_eff(s, f, nused):
        return jnp.where(s < nused[0], f, n_f - 1)

    o_map = lambda s, f, sbe, nsub, base, nused, tok: (s, 0)
    gu_map = lambda s, f, sbe, nsub, base, nused, tok: (0, sbe[s_eff(s, nused)], 0, f_eff(s, f, nused))
    d_map = lambda s, f, sbe, nsub, base, nused, tok: (0, sbe[s_eff(s, nused)], f_eff(s, f, nused), 0)
    grid_spec = pltpu.PrefetchScalarGridSpec(
        num_scalar_prefetch=5,
        grid=(n_super, n_f),
        in_specs=[pl.BlockSpec(memory_space=pl.ANY),
                  pl.BlockSpec((None, None, d, MOE_FC), gu_map),
                  pl.BlockSpec((None, None, d, MOE_FC), gu_map),
                  pl.BlockSpec((None, None, MOE_FC, d), d_map)],
        out_specs=pl.BlockSpec((MOE_SUPER, d), o_map),
        scratch_shapes=[pltpu.VMEM((2, MOE_SUPER, d), F32),
                        pltpu.SemaphoreType.DMA((2,)),
                        pltpu.VMEM((MOE_SUPER, d), F32),
                        pltpu.VMEM((d, MOE_FC), BF16),
                        pltpu.VMEM((d, MOE_FC), BF16),
                        pltpu.VMEM((MOE_FC, d), BF16)],
    )
    return pl.pallas_call(
        functools.partial(_moe_kernel, n_f=n_f),
        grid_spec=grid_spec,
        out_shape=jax.ShapeDtypeStruct((n_super * MOE_SUPER, d), BF16),
        compiler_params=_cparams(("arbitrary", "arbitrary")),
        name="moe_experts",
    )(sb_expert, sb_nsub, sb_base, n_used, sorted_tok, h2, w_g, w_u, w_d)


def _dispatch_tables(eid, n_super):
    flat = eid.reshape(-1)
    order = jnp.argsort(flat)
    inv = jnp.argsort(order)
    sorted_tok = (order // TOP_K).astype(jnp.int32)
    counts = jnp.bincount(flat, length=N_EXPERTS)
    nsup = (counts + MOE_SUPER - 1) // MOE_SUPER
    sup_end = jnp.cumsum(nsup)
    sup_start = sup_end - nsup
    start = jnp.cumsum(counts) - counts
    dest = (sup_start[flat] * MOE_SUPER + inv - start[flat]).astype(jnp.int32)
    sidx = jnp.arange(n_super)
    sb_expert = jnp.minimum(jnp.searchsorted(sup_end, sidx, side='right'), N_EXPERTS - 1)
    n_used = sup_end[-1]
    k = sidx - sup_start[sb_expert]
    rows_in = jnp.clip(counts[sb_expert] - k * MOE_SUPER, 0, MOE_SUPER)
    sb_nsub = jnp.where(sidx < n_used, (rows_in + MOE_SUB - 1) // MOE_SUB, 0)
    sb_base = start[sb_expert] + k * MOE_SUPER
    i32 = lambda a: a.astype(jnp.int32)
    return (sorted_tok, dest.reshape(-1, TOP_K), i32(sb_expert), i32(sb_nsub), i32(sb_base),
            i32(n_used).reshape(1))


def _final_kernel(x1_ref, ya_ref, yb_ref, w_ref, g2_ref, fg_ref, o_ref):
    w = w_ref[0]
    moe = w[:, 0:1] * ya_ref[0].astype(F32) + w[:, 1:2] * yb_ref[0].astype(F32)
    x2 = x1_ref[0] + g2_ref[0] * moe
    o_ref[0] = x2 * lax.rsqrt(jnp.mean(x2 * x2, axis=-1, keepdims=True) + EPS) * fg_ref[...]


def _final(x1, ya, yb, w_top, g2, final_g):
    b, s, d = x1.shape
    ts = 512
    row = lambda bi, i: (bi, i, 0)
    return pl.pallas_call(
        _final_kernel,
        grid=(b, s // ts),
        in_specs=[pl.BlockSpec((1, ts, d), row),
                  pl.BlockSpec((1, ts, d), row),
                  pl.BlockSpec((1, ts, d), row),
                  pl.BlockSpec((1, ts, TOP_K), row),
                  pl.BlockSpec((1, 1, d), lambda bi, i: (bi, 0, 0)),
                  pl.BlockSpec((1, d), lambda bi, i: (0, 0))],
        out_specs=pl.BlockSpec((1, ts, d), row),
        out_shape=jax.ShapeDtypeStruct((b, s, d), F32),
        compiler_params=_cparams(("parallel", "parallel")),
        name="final_norm",
    )(x1, ya, yb, w_top, g2, final_g.reshape(1, d))


def kernel(x, c, ctx, c_ctx, w_mod, b_mod, norm1_g, w_in, mlstm_gate_b, mlstm_norm_g, attn_sink, w_br_m, w_br_a, w_out, norm2_g, w_router_grp, b_router_grp, w_router_exp, b_router_exp, w_exp_gate, w_exp_up, w_exp_down, final_norm_g):
    b, s, d = x.shape
    n_ctx = ctx.shape[1]
    assert w_mod.shape[0] == 1, "single-layer block"

    pad_rows = 8 - (b + 1)
    cvec = jnp.concatenate([c, c_ctx[None, :], jnp.zeros((pad_rows, d), F32)], axis=0)
    mods = _adaln(cvec, w_mod[0], b_mod[0])
    sh1, sc1, g1, sh2, sc2, g2 = [mods[:b, i * d:(i + 1) * d].reshape(b, 1, d) for i in range(6)]
    sh1c = jnp.broadcast_to(mods[b, 0:d].reshape(1, 1, d), (b, 1, d))
    sc1c = jnp.broadcast_to(mods[b, d:2 * d].reshape(1, 1, d), (b, 1, d))

    w = w_in[0]
    w_lat = jnp.concatenate([w[:, :OFF_GM].astype(BF16), w[:, OFF_QA:].astype(BF16)], axis=1)
    w_ctx = jnp.concatenate([w[:, OFF_KM:OFF_OM].astype(BF16), w[:, OFF_KA:OFF_GBR].astype(BF16)], axis=1)
    w_gate = jnp.pad(w[:, OFF_GM:OFF_QA], ((0, 0), (0, LANES - GATE_W))).astype(BF16)

    b_gate = jnp.pad(mlstm_gate_b[0].astype(F32).reshape(1, GATE_W), ((0, 0), (0, LANES - GATE_W)))

    p_lat = _norm_proj(x, norm1_g[0], sh1, sc1, w_lat, BF16, 1024, 512)
    g_lat = _gate_proj(x, norm1_g[0], sh1, sc1, w_gate, b_gate)
    p_ctx = _norm_proj(ctx, norm1_g[0], sh1c, sc1c, w_ctx, BF16, n_ctx, 512)
    g_ctx = _gate_proj(ctx, norm1_g[0], sh1c, sc1c, w_gate, b_gate)

    t_all = n_ctx + s
    gates = jnp.concatenate([g_ctx[:, :, :GATE_W], g_lat[:, :, :GATE_W]], axis=1)
    gates = gates.reshape(b, t_all // M_CHUNK, M_CHUNK, 4, M_HEADS)
    gcol = jnp.transpose(gates, (0, 4, 1, 2, 3))
    grow = jnp.transpose(gates, (0, 4, 1, 3, 2))

    hm = _mlstm(p_lat, p_ctx, gcol, grow, mlstm_norm_g[0])
    cos_t, sin_s = _rope_tables(s)
    att = _attention(p_lat, p_ctx, attn_sink[0], cos_t, sin_s)

    merged = _merge(hm.reshape(b * s, M_V_W), att.reshape(b * s, A_Q_W), p_lat.reshape(b * s, P_WIDTH),
                    w_br_m[0].astype(BF16), w_br_a[0].astype(BF16))

    w_r = jnp.concatenate([w_router_grp[0], w_router_exp[0]], axis=1)
    n_r = w_r.shape[1]
    w_r = jnp.pad(w_r, ((0, 0), (0, LANES - n_r))).astype(BF16)
    b_r = jnp.pad(jnp.concatenate([b_router_grp[0], b_router_exp[0]]), (0, LANES - n_r)).reshape(1, LANES)
    x1, h2, r_idx, r_w = _outproj(merged.reshape(b, s, d), x, w_out[0].astype(BF16), g1, norm2_g[0],
                                  sh2, sc2, w_r, b_r)

    t_tok = b * s
    eid = r_idx.reshape(t_tok, LANES)[:, :TOP_K]
    w_top = r_w[:, :, :TOP_K]
    n_super = (t_tok * TOP_K) // MOE_SUPER + N_EXPERTS
    sorted_tok, dest, sb_expert, sb_nsub, sb_base, n_used = _dispatch_tables(eid, n_super)
    ybuf = _moe_experts(h2.reshape(t_tok, d), sorted_tok, sb_expert, sb_nsub, sb_base, n_used, n_super,
                        w_exp_gate, w_exp_up, w_exp_down)
    ya = ybuf.at[dest[:, 0]].get(mode="promise_in_bounds").reshape(b, s, d)
    yb = ybuf.at[dest[:, 1]].get(mode="promise_in_bounds").reshape(b, s, d)

    return _final(x1, ya, yb, w_top, g2, final_norm_g)
```

```python
import functools

import jax
import jax.numpy as jnp
from jax import lax
from jax.experimental import pallas as pl
from jax.experimental.pallas import tpu as pltpu

F32 = jnp.float32
BF16 = jnp.bfloat16

D_MODEL = 2048
EPS = 1e-6
GRID_W = 64

M_HEADS = 8
M_DQK = 128
M_DV = 256
M_CHUNK = 256
assert M_DQK == 128, "mLSTM state rows are laid out one key dim per lane"

A_HEADS = 16
A_KV_HEADS = 4
A_GROUP = A_HEADS // A_KV_HEADS
A_DH = 128
A_BLOCK = 128
WINDOW = 128
ROPE_HALF = A_DH // 4
ROPE_BASE = 10000.0
assert WINDOW == A_BLOCK, "band masks assume the window equals the block size"

N_GROUPS = 8
EXPERTS_PER_GROUP = 8
N_EXPERTS = N_GROUPS * EXPERTS_PER_GROUP
TOP_K = 2
D_EXPERT = 1024

M_QK_W = M_HEADS * M_DQK
M_V_W = M_HEADS * M_DV
A_Q_W = A_HEADS * A_DH
A_KV_W = A_KV_HEADS * A_DH
GATE_W = 4 * M_HEADS

OFF_QM = 0
OFF_KM = OFF_QM + M_QK_W
OFF_VM = OFF_KM + M_QK_W
OFF_OM = OFF_VM + M_V_W
OFF_GM = OFF_OM + M_V_W
OFF_QA = OFF_GM + GATE_W
OFF_KA = OFF_QA + A_Q_W
OFF_VA = OFF_KA + A_KV_W
OFF_GBR = OFF_VA + A_KV_W
IN_WIDTH = OFF_GBR + 2 * D_MODEL

P_QM = 0
P_KM = P_QM + M_QK_W
P_VM = P_KM + M_QK_W
P_OM = P_VM + M_V_W
P_QA = P_OM + M_V_W
P_KA = P_QA + A_Q_W
P_VA = P_KA + A_KV_W
P_GBM = P_VA + A_KV_W
P_GBA = P_GBM + D_MODEL
P_WIDTH = P_GBA + D_MODEL

LANES = 128
NORM_ROWS = 512
PROJ_TM = 1024
PROJ_TN = 1024
MOE_SUB = 128
MOE_SUPER = 512
MOE_FC = 512
GATHER_UNROLL = 8
VMEM_LIMIT = 56 * 1024 * 1024


def _cparams(sem, vmem=VMEM_LIMIT):
    return pltpu.CompilerParams(dimension_semantics=sem, vmem_limit_bytes=vmem)


def _adaln_kernel(c_ref, w_ref, b_ref, o_ref):
    c = c_ref[...]
    a = (c * jax.nn.sigmoid(c)).astype(BF16)
    o_ref[...] = jnp.dot(a, w_ref[...].astype(BF16), preferred_element_type=F32) + b_ref[...]


def _adaln(cvec, w_mod, b_mod):
    rows, d = cvec.shape
    n = w_mod.shape[1]
    tn = 1024
    return pl.pallas_call(
        _adaln_kernel,
        grid=(n // tn,),
        in_specs=[pl.BlockSpec((rows, d), lambda j: (0, 0)),
                  pl.BlockSpec((d, tn), lambda j: (0, j)),
                  pl.BlockSpec((1, tn), lambda j: (0, j))],
        out_specs=pl.BlockSpec((rows, tn), lambda j: (0, j)),
        out_shape=jax.ShapeDtypeStruct((rows, n), F32),
        compiler_params=_cparams(("parallel",)),
        name="adaln",
    )(cvec, w_mod, b_mod.reshape(1, n))


def _norm_mod_kernel(x_ref, c_ref, g_ref, sh_ref, sc_ref, o_ref, *, n_lat_tiles):
    xv = jnp.where(pl.program_id(0) < n_lat_tiles, x_ref[...], c_ref[...])
    y = xv * lax.rsqrt(jnp.mean(xv * xv, axis=-1, keepdims=True) + EPS) * g_ref[...]
    o_ref[...] = (y * (1.0 + sc_ref[0]) + sh_ref[0]).astype(o_ref.dtype)


def _norm_mod(x2d, ctx2d, g, shift, scale, rows_per_batch, ctx_mod_row):
    t_lat, d = x2d.shape
    t_ctx = ctx2d.shape[0]
    tm = NORM_ROWS
    n_lat, n_ctx = t_lat // tm, t_ctx // tm
    tiles_per_batch = rows_per_batch // tm
    mod_map = lambda i: (jnp.where(i < n_lat, i // tiles_per_batch, ctx_mod_row), 0, 0)
    return pl.pallas_call(
        functools.partial(_norm_mod_kernel, n_lat_tiles=n_lat),
        grid=(n_lat + n_ctx,),
        in_specs=[pl.BlockSpec((tm, d), lambda i: (jnp.minimum(i, n_lat - 1), 0)),
                  pl.BlockSpec((tm, d), lambda i: (jnp.maximum(i - n_lat, 0), 0)),
                  pl.BlockSpec((1, d), lambda i: (0, 0)),
                  pl.BlockSpec((1, 1, d), mod_map),
                  pl.BlockSpec((1, 1, d), mod_map)],
        out_specs=pl.BlockSpec((tm, d), lambda i: (i, 0)),
        out_shape=jax.ShapeDtypeStruct((t_lat + t_ctx, d), BF16),
        compiler_params=_cparams(("parallel",)),
        name="norm_mod",
    )(x2d, ctx2d, g.reshape(1, d), shift, scale)


def _proj_kernel(h_ref, wa_ref, wb_ref, o_ref, w_bf, *, first_shifted):
    j = pl.program_id(0)

    @pl.when(pl.program_id(1) == 0)
    def _():
        @pl.when(j < first_shifted)
        def _():
            w_bf[...] = wa_ref[...].astype(BF16)

        @pl.when(j >= first_shifted)
        def _():
            w_bf[...] = jnp.concatenate([wa_ref[:, GATE_W:], wb_ref[:, :GATE_W]], axis=1).astype(BF16)

    o_ref[...] = jnp.dot(h_ref[...], w_bf[...], preferred_element_type=F32).astype(o_ref.dtype)


def _proj(h_all, w_in2d):
    t, d = h_all.shape
    tm, tn = PROJ_TM, PROJ_TN
    assert OFF_GM % tn == 0 and P_WIDTH % tn == 0 and t % tm == 0
    return pl.pallas_call(
        functools.partial(_proj_kernel, first_shifted=OFF_GM // tn),
        grid=(P_WIDTH // tn, t // tm),
        in_specs=[pl.BlockSpec((tm, d), lambda j, i: (i, 0)),
                  pl.BlockSpec((d, tn), lambda j, i: (0, j)),
                  pl.BlockSpec((d, LANES), lambda j, i: (0, (j + 1) * (tn // LANES)))],
        out_specs=pl.BlockSpec((tm, tn), lambda j, i: (i, j)),
        out_shape=jax.ShapeDtypeStruct((t, P_WIDTH), BF16),
        scratch_shapes=[pltpu.VMEM((d, tn), BF16)],
        compiler_params=_cparams(("parallel", "arbitrary")),
        name="in_proj",
    )(h_all, w_in2d, w_in2d)


def _log_sigmoid(x):
    return jnp.minimum(x, 0.0) - jnp.log(1.0 + jnp.exp(-jnp.abs(x)))


def _gate_proj_kernel(h_ref, w_ref, b_ref, o_ref):
    wl = lax.broadcasted_iota(jnp.int32, w_ref.shape, 1)
    w = jnp.where(wl < GATE_W, w_ref[...], 0.0).astype(BF16)
    g = jnp.dot(h_ref[...], w, preferred_element_type=F32) + b_ref[...]
    lf = _log_sigmoid(g)
    L = g.shape[0]
    row = lax.broadcasted_iota(jnp.int32, (L, L), 0)
    col = lax.broadcasted_iota(jnp.int32, (L, L), 1)
    hi = lax.Precision.HIGHEST
    cum_f = jnp.dot((col <= row).astype(F32), lf, precision=hi, preferred_element_type=F32)
    cum_b = jnp.dot((col >= row).astype(F32), lf, precision=hi, preferred_element_type=F32)
    lane = lax.broadcasted_iota(jnp.int32, g.shape, 1)
    is_ff = (lane >= M_HEADS) & (lane < 2 * M_HEADS)
    is_fb = (lane >= 3 * M_HEADS) & (lane < 4 * M_HEADS)
    o_ref[...] = jnp.where(is_ff, cum_f, jnp.where(is_fb, cum_b, g))


def _gate_proj(h_all, w_in2d, b_gate):
    t, d = h_all.shape
    L = M_CHUNK
    assert OFF_GM % LANES == 0
    return pl.pallas_call(
        _gate_proj_kernel,
        grid=(t // L,),
        in_specs=[pl.BlockSpec((L, d), lambda i: (i, 0)),
                  pl.BlockSpec((d, LANES), lambda i: (0, OFF_GM // LANES)),
                  pl.BlockSpec((1, LANES), lambda i: (0, 0))],
        out_specs=pl.BlockSpec((L, LANES), lambda i: (i, 0)),
        out_shape=jax.ShapeDtypeStruct((t, LANES), F32),
        compiler_params=_cparams(("parallel",)),
        name="gate_proj",
    )(h_all, w_in2d, b_gate)


def _mlstm_chunk(direction, qv, kv, ktv, vv, gcol, grow, c_ref, n_ref, m_ref, valid):
    gi = 2 * direction
    igr = grow[gi:gi + 1, :]
    b_r = grow[gi + 1:gi + 2, :]
    L = b_r.shape[1]
    igc = jnp.broadcast_to(gcol[:, gi:gi + 1], (L, LANES))
    b_c = jnp.broadcast_to(gcol[:, gi + 1:gi + 2], (L, LANES))
    wide = lambda a, n: jnp.concatenate([a] * (n // LANES), axis=1)
    b_end = b_r[:, L - 1:L] if direction == 0 else b_r[:, 0:1]
    kscale = M_DQK ** -0.5

    c_st = c_ref[direction]
    n_st = n_ref[direction]
    m_st = m_ref[direction]

    h = None
    if qv is not None:
        dm = jnp.where(valid, wide(b_c, L) - b_r + igr, -jnp.inf)
        inter = b_c + m_st
        m_t = jnp.maximum(inter, jnp.max(dm, axis=1, keepdims=True))
        w_inter = jnp.exp(inter - m_t)
        sc = lax.dot_general(qv, kv, (((1,), (1,)), ((), ())), preferred_element_type=F32) * kscale
        s = sc * jnp.exp(dm - wide(m_t, L))
        num = (wide(w_inter, M_DV) * jnp.dot(qv, c_st.astype(BF16), preferred_element_type=F32)
               + jnp.dot(s.astype(BF16), vv, preferred_element_type=F32))
        qn = (w_inter * jnp.sum(qv.astype(F32) * n_st, axis=1, keepdims=True)
              + jnp.sum(s, axis=1, keepdims=True))
        h = num * wide(1.0 / jnp.maximum(jnp.abs(qn), jnp.exp(-m_t)), M_DV)

    dec = b_end - b_c + igc
    m_new = jnp.maximum(b_end + m_st, jnp.max(dec, axis=0, keepdims=True))
    w_s = jnp.exp(dec - m_new) * kscale
    w_c = jnp.exp(b_end + m_st - m_new)
    wv = (wide(w_s, M_DV) * vv.astype(F32)).astype(BF16)
    c_ref[direction] = wide(w_c, M_DV) * c_st + jnp.dot(ktv, wv, preferred_element_type=F32)
    n_ref[direction] = w_c * n_st + jnp.sum(w_s * kv.astype(F32), axis=0, keepdims=True)
    m_ref[direction] = m_new
    return h


def _mlstm_kernel(q_ref, k_ref, kt_ref, v_ref, o_ref, kc_ref, ktc_ref, vc_ref, gcol_ref, grow_ref, ng_ref,
                  out_ref, c_ref, n_ref, m_ref, hf_ref, hb_ref, *, n_ctx_chunks, n_lat_chunks):
    L = M_CHUNK
    row = lax.broadcasted_iota(jnp.int32, (L, L), 0)
    col = lax.broadcasted_iota(jnp.int32, (L, L), 1)
    valid = (col <= row, col >= row)

    c_ref[...] = jnp.zeros_like(c_ref)
    n_ref[...] = jnp.zeros_like(n_ref)
    m_ref[...] = jnp.zeros_like(m_ref)

    for j in range(n_ctx_chunks):
        for direction, jj in ((0, j), (1, n_ctx_chunks - 1 - j)):
            _mlstm_chunk(direction, None, kc_ref[jj * L:(jj + 1) * L, :], ktc_ref[0, 0, jj],
                         vc_ref[jj * L:(jj + 1) * L, :], gcol_ref[0, 0, jj], grow_ref[0, 0, jj],
                         c_ref, n_ref, m_ref, valid[direction])

    def body(j, carry):
        for direction, jj, h_ref in ((0, j, hf_ref), (1, n_lat_chunks - 1 - j, hb_ref)):
            r0 = pl.multiple_of(jj * L, L)
            h = _mlstm_chunk(direction, q_ref[pl.ds(r0, L), :], k_ref[pl.ds(r0, L), :],
                             kt_ref[0, 0, jj], v_ref[pl.ds(r0, L), :],
                             gcol_ref[0, 0, n_ctx_chunks + jj], grow_ref[0, 0, n_ctx_chunks + jj],
                             c_ref, n_ref, m_ref, valid[direction])
            h_ref[pl.ds(r0, L), :] = h
        return carry

    lax.fori_loop(0, n_lat_chunks, body, 0)

    def finish(j, carry):
        r0 = pl.multiple_of(j * L, L)
        hh = hf_ref[pl.ds(r0, L), :] + hb_ref[pl.ds(r0, L), :]
        hh = hh * lax.rsqrt(jnp.mean(hh * hh, axis=-1, keepdims=True) + EPS)
        og = jax.nn.sigmoid(o_ref[pl.ds(r0, L), :].astype(F32))
        out_ref[pl.ds(r0, L), :] = (hh * ng_ref[...] * og).astype(out_ref.dtype)
        return carry

    lax.fori_loop(0, n_lat_chunks, finish, 0)


def _chunked_kt(k2d, b, n_chunks):
    k = k2d.reshape(b, n_chunks, M_CHUNK, M_HEADS, M_DQK)
    return jnp.transpose(k, (0, 3, 1, 4, 2))


def _mlstm(p_all, b, s, n_ctx, gcol, grow, norm_g):
    L = M_CHUNK
    nck, nlk = n_ctx // L, s // L
    t_lat = b * s
    cb = t_lat // n_ctx
    kt_lat = _chunked_kt(p_all[:t_lat, P_KM:P_KM + M_QK_W], b, nlk)
    kt_ctx = _chunked_kt(p_all[t_lat:, P_KM:P_KM + M_QK_W], b, nck)
    kern = functools.partial(_mlstm_kernel, n_ctx_chunks=nck, n_lat_chunks=nlk)
    return pl.pallas_call(
        kern,
        grid=(b, M_HEADS),
        in_specs=[pl.BlockSpec((s, M_DQK), lambda bi, h: (bi, P_QM // M_DQK + h)),
                  pl.BlockSpec((s, M_DQK), lambda bi, h: (bi, P_KM // M_DQK + h)),
                  pl.BlockSpec((1, 1, nlk, M_DQK, L), lambda bi, h: (bi, h, 0, 0, 0)),
                  pl.BlockSpec((s, M_DV), lambda bi, h: (bi, P_VM // M_DV + h)),
                  pl.BlockSpec((s, M_DV), lambda bi, h: (bi, P_OM // M_DV + h)),
                  pl.BlockSpec((n_ctx, M_DQK), lambda bi, h: (cb + bi, P_KM // M_DQK + h)),
                  pl.BlockSpec((1, 1, nck, M_DQK, L), lambda bi, h: (bi, h, 0, 0, 0)),
                  pl.BlockSpec((n_ctx, M_DV), lambda bi, h: (cb + bi, P_VM // M_DV + h)),
                  pl.BlockSpec((1, 1, nck + nlk, L, 4), lambda bi, h: (bi, h, 0, 0, 0)),
                  pl.BlockSpec((1, 1, nck + nlk, 4, L), lambda bi, h: (bi, h, 0, 0, 0)),
                  pl.BlockSpec((1, M_DV), lambda bi, h: (0, h))],
        out_specs=pl.BlockSpec((s, M_DV), lambda bi, h: (bi, h)),
        out_shape=jax.ShapeDtypeStruct((t_lat, M_V_W), BF16),
        scratch_shapes=[pltpu.VMEM((2, M_DQK, M_DV), F32),
                        pltpu.VMEM((2, 1, M_DQK), F32),
                        pltpu.VMEM((2, 1, LANES), F32),
                        pltpu.VMEM((s, M_DV), F32),
                        pltpu.VMEM((s, M_DV), F32)],
        compiler_params=_cparams(("parallel", "parallel")),
        name="mlstm",
    )(p_all, p_all, kt_lat, p_all, p_all, p_all, kt_ctx, p_all, gcol, grow, norm_g.reshape(1, M_V_W))


def _rope(t, cos_t, sin_s, lane_lo):
    swapped = jnp.where(lane_lo, pltpu.roll(t, A_DH - ROPE_HALF, 1), pltpu.roll(t, ROPE_HALF, 1))
    return t * cos_t + swapped * sin_s


def _attn_kernel(sink_ref, q_ref, k_ref, v_ref, kc_ref, vc_ref, cos_ref, sin_ref, out_ref, kpad, vpad):
    s = k_ref.shape[0]
    blk = A_BLOCK
    hk = pl.program_id(1)
    scale = A_DH ** -0.5
    lane = lax.broadcasted_iota(jnp.int32, (1, A_DH), 1)
    lane_lo = (lane % (2 * ROPE_HALF)) < ROPE_HALF

    zero = jnp.zeros((blk, A_DH), BF16)
    kpad[0:blk, :] = zero
    kpad[s + blk:s + 2 * blk, :] = zero
    vpad[0:blk, :] = zero
    vpad[s + blk:s + 2 * blk, :] = zero
    vpad[blk:s + blk, :] = v_ref[...]

    def rope_k(n, carry):
        r0 = pl.multiple_of(n * blk, blk)
        kf = k_ref[pl.ds(r0, blk), :].astype(F32)
        kr = _rope(kf, cos_ref[pl.ds(r0, blk), :], sin_ref[pl.ds(r0, blk), :], lane_lo)
        kpad[pl.ds(r0 + blk, blk), :] = kr.astype(BF16)
        return carry

    lax.fori_loop(0, s // blk, rope_k, 0)

    kc = kc_ref[...]
    vc = vc_ref[...]
    n_ctx = kc.shape[0]
    n_blocks = s // blk
    rr = lax.broadcasted_iota(jnp.int32, (blk, blk), 0)
    cc = lax.broadcasted_iota(jnp.int32, (blk, blk), 1)
    nt = (((1,), (1,)), ((), ()))
    log2e = 1.4426950408889634
    neg = -jnp.inf

    def body(n, carry):
        r0 = pl.multiple_of(n * blk, blk)
        cos_q = cos_ref[pl.ds(r0, blk), :]
        sin_q = sin_ref[pl.ds(r0, blk), :]
        kb = kpad[pl.ds(r0, 3 * blk), :]
        vb = vpad[pl.ds(r0, 3 * blk), :]
        bias_prev = jnp.where(cc >= rr, jnp.where(n > 0, 0.0, neg), neg)
        bias_next = jnp.where(cc <= rr, jnp.where(n < n_blocks - 1, 0.0, neg), neg)
        q_all = jnp.concatenate(
            [_rope(q_ref[pl.ds(r0, blk), g * A_DH:(g + 1) * A_DH].astype(F32), cos_q, sin_q,
                   lane_lo).astype(BF16) for g in range(A_GROUP)], axis=0)
        s_loc = lax.dot_general(q_all, kb, nt, preferred_element_type=F32)
        s_ctx = lax.dot_general(q_all, kc, nt, preferred_element_type=F32)
        p_parts, inv_den = [], []
        for g in range(A_GROUP):
            rows = slice(g * blk, (g + 1) * blk)
            tiles = ([s_loc[rows, 0:blk] + bias_prev, s_loc[rows, blk:2 * blk],
                      s_loc[rows, 2 * blk:3 * blk] + bias_next]
                     + [s_ctx[rows, c:c + blk] for c in range(0, n_ctx, blk)])
            tmax = functools.reduce(jnp.maximum, tiles)
            snk = sink_ref[hk * A_GROUP + g]
            m = jnp.maximum(jnp.max(tmax, axis=1, keepdims=True) * scale, snk)
            mb = m * log2e
            p = [jnp.exp2(t * (scale * log2e) - mb) for t in tiles]
            den = jnp.sum(functools.reduce(jnp.add, p), axis=1, keepdims=True) + jnp.exp(snk - m)
            inv_den.append(1.0 / den)
            p_parts.append(jnp.concatenate([t.astype(BF16) for t in p], axis=1))
        p_all = jnp.concatenate(p_parts, axis=0)
        o_all = (jnp.dot(p_all[:, :3 * blk], vb, preferred_element_type=F32)
                 + jnp.dot(p_all[:, 3 * blk:], vc, preferred_element_type=F32))
        for g in range(A_GROUP):
            o = o_all[g * blk:(g + 1) * blk, :] * inv_den[g]
            out_ref[pl.ds(r0, blk), g * A_DH:(g + 1) * A_DH] = o.astype(out_ref.dtype)
        return carry

    lax.fori_loop(0, n_blocks, body, 0)


def _attention(p_all, b, s, n_ctx, sink, cos_t, sin_s):
    gw = A_GROUP * A_DH
    cb = (b * s) // n_ctx
    return pl.pallas_call(
        _attn_kernel,
        grid=(b, A_KV_HEADS),
        in_specs=[pl.BlockSpec(memory_space=pltpu.SMEM),
                  pl.BlockSpec((s, gw), lambda bi, h: (bi, P_QA // gw + h)),
                  pl.BlockSpec((s, A_DH), lambda bi, h: (bi, P_KA // A_DH + h)),
                  pl.BlockSpec((s, A_DH), lambda bi, h: (bi, P_VA // A_DH + h)),
                  pl.BlockSpec((n_ctx, A_DH), lambda bi, h: (cb + bi, P_KA // A_DH + h)),
                  pl.BlockSpec((n_ctx, A_DH), lambda bi, h: (cb + bi, P_VA // A_DH + h)),
                  pl.BlockSpec((s, A_DH), lambda bi, h: (0, 0)),
                  pl.BlockSpec((s, A_DH), lambda bi, h: (0, 0))],
        out_specs=pl.BlockSpec((s, gw), lambda bi, h: (bi, h)),
        out_shape=jax.ShapeDtypeStruct((b * s, A_Q_W), BF16),
        scratch_shapes=[pltpu.VMEM((s + 2 * A_BLOCK, A_DH), BF16),
                        pltpu.VMEM((s + 2 * A_BLOCK, A_DH), BF16)],
        compiler_params=_cparams(("parallel", "parallel")),
        name="attention",
    )(sink, p_all, p_all, p_all, p_all, p_all, cos_t, sin_s)


def _rope_tables(n_tokens):
    rows = n_tokens // GRID_W
    row = jnp.repeat(jnp.arange(rows, dtype=F32), GRID_W)
    col = jnp.tile(jnp.arange(GRID_W, dtype=F32), rows)
    inv_freq = ROPE_BASE ** (-jnp.arange(ROPE_HALF, dtype=F32) / ROPE_HALF)
    ar = row[:, None] * inv_freq
    ac = col[:, None] * inv_freq
    cos_t = jnp.concatenate([jnp.cos(ar), jnp.cos(ar), jnp.cos(ac), jnp.cos(ac)], axis=1)
    sin_s = jnp.concatenate([-jnp.sin(ar), jnp.sin(ar), -jnp.sin(ac), jnp.sin(ac)], axis=1)
    return cos_t, sin_s


def _merge_kernel(hm_ref, at_ref, wm_ref, wa_ref, gm_ref, ga_ref, o_ref):
    bm = jnp.dot(hm_ref[...], wm_ref[...], preferred_element_type=F32)
    ba = jnp.dot(at_ref[...], wa_ref[...], preferred_element_type=F32)
    o = jax.nn.sigmoid(gm_ref[...].astype(F32)) * bm + jax.nn.sigmoid(ga_ref[...].astype(F32)) * ba
    o_ref[...] = o.astype(o_ref.dtype)


def _merge(hm, att, p2d, w_br_m, w_br_a):
    t, d = hm.shape
    tm, tn = 1024, 512
    return pl.pallas_call(
        _merge_kernel,
        grid=(t // tm, d // tn),
        in_specs=[pl.BlockSpec((tm, M_V_W), lambda i, j: (i, 0)),
                  pl.BlockSpec((tm, A_Q_W), lambda i, j: (i, 0)),
                  pl.BlockSpec((M_V_W, tn), lambda i, j: (0, j)),
                  pl.BlockSpec((A_Q_W, tn), lambda i, j: (0, j)),
                  pl.BlockSpec((tm, tn), lambda i, j: (i, P_GBM // tn + j)),
                  pl.BlockSpec((tm, tn), lambda i, j: (i, P_GBA // tn + j))],
        out_specs=pl.BlockSpec((tm, tn), lambda i, j: (i, j)),
        out_shape=jax.ShapeDtypeStruct((t, d), BF16),
        compiler_params=_cparams(("parallel", "arbitrary")),
        name="merge",
    )(hm, att, w_br_m, w_br_a, p2d, p2d)


def _outproj_kernel(mg_ref, x_ref, wo_ref, g1_ref, ng_ref, sh_ref, sc_ref, wr_ref, br_ref,
                    x1_ref, h2_ref, ri_ref, rw_ref):
    y = jnp.dot(mg_ref[0], wo_ref[...], preferred_element_type=F32)
    x1 = x_ref[0] + g1_ref[0] * y
    x1_ref[0] = x1
    hn = x1 * lax.rsqrt(jnp.mean(x1 * x1, axis=-1, keepdims=True) + EPS) * ng_ref[...]
    h2 = hn * (1.0 + sc_ref[0]) + sh_ref[0]
    h2_ref[0] = h2

    logits = jnp.dot(h2.astype(BF16), wr_ref[...], preferred_element_type=F32) + br_ref[...]
    lane = lax.broadcasted_iota(jnp.int32, logits.shape, 1)
    neg = -jnp.inf
    gl = jnp.where(lane < N_GROUPS, logits, neg)
    gmax = jnp.max(gl, axis=1, keepdims=True)
    gsel = jnp.min(jnp.where(gl == gmax, lane, LANES), axis=1, keepdims=True)
    p_sel = 1.0 / jnp.sum(jnp.exp(gl - gmax), axis=1, keepdims=True)
    lo = N_GROUPS + gsel * EXPERTS_PER_GROUP
    el = jnp.where((lane >= lo) & (lane < lo + EXPERTS_PER_GROUP), logits, neg)
    v1 = jnp.max(el, axis=1, keepdims=True)
    i1 = jnp.min(jnp.where(el == v1, lane, LANES), axis=1, keepdims=True)
    el2 = jnp.where(lane == i1, neg, el)
    v2 = jnp.max(el2, axis=1, keepdims=True)
    i2 = jnp.min(jnp.where(el2 == v2, lane, LANES), axis=1, keepdims=True)
    e2 = jnp.exp(v2 - v1)
    w1 = p_sel / (1.0 + e2)
    w2 = p_sel * e2 / (1.0 + e2)
    ri_ref[0] = jnp.where(lane == 0, i1 - N_GROUPS, jnp.where(lane == 1, i2 - N_GROUPS, 0))
    rw_ref[0] = jnp.where(lane == 0, w1, jnp.where(lane == 1, w2, 0.0))


def _outproj(merged, x, w_out, g1, norm_g, shift, scale, w_r, b_r):
    b, s, d = x.shape
    tm = 512
    row = lambda bi, i: (bi, i, 0)
    per_b = lambda bi, i: (bi, 0, 0)
    fixed = lambda bi, i: (0, 0)
    return pl.pallas_call(
        _outproj_kernel,
        grid=(b, s // tm),
        in_specs=[pl.BlockSpec((1, tm, d), row),
                  pl.BlockSpec((1, tm, d), row),
                  pl.BlockSpec((d, d), fixed),
                  pl.BlockSpec((1, 1, d), per_b),
                  pl.BlockSpec((1, d), fixed),
                  pl.BlockSpec((1, 1, d), per_b),
                  pl.BlockSpec((1, 1, d), per_b),
                  pl.BlockSpec((d, LANES), fixed),
                  pl.BlockSpec((1, LANES), fixed)],
        out_specs=[pl.BlockSpec((1, tm, d), row),
                   pl.BlockSpec((1, tm, d), row),
                   pl.BlockSpec((1, tm, LANES), row),
                   pl.BlockSpec((1, tm, LANES), row)],
        out_shape=[jax.ShapeDtypeStruct((b, s, d), F32),
                   jax.ShapeDtypeStruct((b, s, d), F32),
                   jax.ShapeDtypeStruct((b, s, LANES), jnp.int32),
                   jax.ShapeDtypeStruct((b, s, LANES), F32)],
        compiler_params=_cparams(("parallel", "parallel")),
        name="outproj_router",
    )(merged, x, w_out, g1, norm_g.reshape(1, d), shift, scale, w_r, b_r)


def _moe_kernel(sbe_ref, nsub_ref, base_ref, nused_ref, tok_ref, h_hbm, wg_ref, wu_ref, wd_ref, o_ref,
                xbuf, sem, acc_ref, wgb, wub, wdb, *, n_f):
    s = pl.program_id(0)
    f = pl.program_id(1)
    nused = nused_ref[0]
    n_assign = tok_ref.shape[0]

    def start_gather(sb, slot):
        base = base_ref[sb]

        def issue(i, carry):
            for u in range(GATHER_UNROLL):
                r = i * GATHER_UNROLL + u
                tok = tok_ref[jnp.minimum(base + r, n_assign - 1)]
                pltpu.make_async_copy(h_hbm.at[pl.ds(tok, 1)], xbuf.at[slot, pl.ds(r, 1)],
                                      sem.at[slot]).start()
            return carry

        lax.fori_loop(0, nsub_ref[sb] * (MOE_SUB // GATHER_UNROLL), issue, 0)

    def wait_gather(sb, slot):
        def wait_sub(i, carry):
            pltpu.make_async_copy(h_hbm.at[pl.ds(0, MOE_SUB)], xbuf.at[slot, pl.ds(0, MOE_SUB)],
                                  sem.at[slot]).wait()
            return carry

        lax.fori_loop(0, nsub_ref[sb], wait_sub, 0)

    @pl.when((s >= nused) & (f == 0))
    def _():
        o_ref[...] = jnp.zeros_like(o_ref)

    @pl.when(s < nused)
    def _():
        slot = s % 2

        @pl.when(f == 0)
        def _():
            @pl.when(s == 0)
            def _():
                start_gather(0, 0)

            wait_gather(s, slot)

            @pl.when(s + 1 < nused)
            def _():
                start_gather(s + 1, 1 - slot)

            acc_ref[...] = jnp.zeros_like(acc_ref)

        wgb[...] = wg_ref[...].astype(BF16)
        wub[...] = wu_ref[...].astype(BF16)
        wdb[...] = wd_ref[...].astype(BF16)

        def body(i, carry):
            r0 = pl.multiple_of(i * MOE_SUB, MOE_SUB)
            xb = xbuf[slot, pl.ds(r0, MOE_SUB), :].astype(BF16)
            hg = jnp.dot(xb, wgb[...], preferred_element_type=F32)
            hu = jnp.dot(xb, wub[...], preferred_element_type=F32)
            a = (hg * jax.nn.sigmoid(hg) * hu).astype(BF16)
            acc_ref[pl.ds(r0, MOE_SUB), :] += jnp.dot(a, wdb[...], preferred_element_type=F32)
            return carry

        lax.fori_loop(0, nsub_ref[s], body, 0)

        @pl.when(f == n_f - 1)
        def _():
            o_ref[...] = acc_ref[...].astype(o_ref.dtype)


def _moe_experts(h2, sorted_tok, sb_expert, sb_nsub, sb_base, n_used, n_super, w_g, w_u, w_d):
    _, d = h2.shape
    n_f = D_EXPERT // MOE_FC

    def s_eff(s, nused):
        return jnp.minimum(s, nused[0] - 1)

    def f_eff(s, f, nused):
        return jnp.where(s < nused[0], f, n_f - 1)

    o_map = lambda s, f, sbe, nsub, base, nused, tok: (s, 0)
    gu_map = lambda s, f, sbe, nsub, base, nused, tok: (0, sbe[s_eff(s, nused)], 0, f_eff(s, f, nused))
    d_map = lambda s, f, sbe, nsub, base, nused, tok: (0, sbe[s_eff(s, nused)], f_eff(s, f, nused), 0)
    grid_spec = pltpu.PrefetchScalarGridSpec(
        num_scalar_prefetch=5,
        grid=(n_super, n_f),
        in_specs=[pl.BlockSpec(memory_space=pl.ANY),
                  pl.BlockSpec((None, None, d, MOE_FC), gu_map),
                  pl.BlockSpec((None, None, d, MOE_FC), gu_map),
                  pl.BlockSpec((None, None, MOE_FC, d), d_map)],
        out_specs=pl.BlockSpec((MOE_SUPER, d), o_map),
        scratch_shapes=[pltpu.VMEM((2, MOE_SUPER, d), F32),
                        pltpu.SemaphoreType.DMA((2,)),
                        pltpu.VMEM((MOE_SUPER, d), F32),
                        pltpu.VMEM((d, MOE_FC), BF16),
                        pltpu.VMEM((d, MOE_FC), BF16),
                        pltpu.VMEM((MOE_FC, d), BF16)],
    )
    return pl.pallas_call(
        functools.partial(_moe_kernel, n_f=n_f),
        grid_spec=grid_spec,
        out_shape=jax.ShapeDtypeStruct((n_super * MOE_SUPER, d), BF16),
        compiler_params=_cparams(("arbitrary", "arbitrary")),
        name="moe_experts",
    )(sb_expert, sb_nsub, sb_base, n_used, sorted_tok, h2, w_g, w_u, w_d)


def _dispatch_tables(eid, n_super):
    flat = eid.reshape(-1)
    n_assign = flat.shape[0]
    order = jnp.argsort(flat).astype(jnp.int32)
    sorted_tok = order // TOP_K
    counts = jnp.bincount(flat, length=N_EXPERTS)
    nsup = (counts + MOE_SUPER - 1) // MOE_SUPER
    sup_end = jnp.cumsum(nsup)
    sup_start = sup_end - nsup
    start = jnp.cumsum(counts) - counts
    off = sup_start * MOE_SUPER - start
    d_off = jnp.diff(off, prepend=0)
    pos = jnp.arange(n_assign)
    off_sorted = jnp.sum(jnp.where(pos[:, None] >= start[None, :], d_off[None, :], 0), axis=1)
    dest_sorted = (pos + off_sorted).astype(jnp.int32)
    _, dest = lax.sort((order, dest_sorted), num_keys=1)
    sidx = jnp.arange(n_super)
    sb_expert = jnp.minimum(jnp.sum(sup_end[None, :] <= sidx[:, None], axis=1), N_EXPERTS - 1)
    n_used = sup_end[-1]
    k = sidx - sup_start[sb_expert]
    rows_in = jnp.clip(counts[sb_expert] - k * MOE_SUPER, 0, MOE_SUPER)
    sb_nsub = jnp.where(sidx < n_used, (rows_in + MOE_SUB - 1) // MOE_SUB, 0)
    sb_base = start[sb_expert] + k * MOE_SUPER
    i32 = lambda a: a.astype(jnp.int32)
    return (sorted_tok, dest.reshape(-1, TOP_K), i32(sb_expert), i32(sb_nsub), i32(sb_base),
            i32(n_used).reshape(1))


def _final_kernel(x1_ref, ya_ref, yb_ref, w_ref, g2_ref, fg_ref, o_ref):
    w = w_ref[0]
    moe = w[:, 0:1] * ya_ref[0].astype(F32) + w[:, 1:2] * yb_ref[0].astype(F32)
    x2 = x1_ref[0] + g2_ref[0] * moe
    o_ref[0] = x2 * lax.rsqrt(jnp.mean(x2 * x2, axis=-1, keepdims=True) + EPS) * fg_ref[...]


def _final(x1, ya, yb, w_top, g2, final_g):
    b, s, d = x1.shape
    ts = 512
    row = lambda bi, i: (bi, i, 0)
    return pl.pallas_call(
        _final_kernel,
        grid=(b, s // ts),
        in_specs=[pl.BlockSpec((1, ts, d), row),
                  pl.BlockSpec((1, ts, d), row),
                  pl.BlockSpec((1, ts, d), row),
                  pl.BlockSpec((1, ts, TOP_K), row),
                  pl.BlockSpec((1, 1, d), lambda bi, i: (bi, 0, 0)),
                  pl.BlockSpec((1, d), lambda bi, i: (0, 0))],
        out_specs=pl.BlockSpec((1, ts, d), row),
        out_shape=jax.ShapeDtypeStruct((b, s, d), F32),
        compiler_params=_cparams(("parallel", "parallel")),
        name="final_norm",
    )(x1, ya, yb, w_top, g2, final_g.reshape(1, d))


def kernel(x, c, ctx, c_ctx, w_mod, b_mod, norm1_g, w_in, mlstm_gate_b, mlstm_norm_g, attn_sink, w_br_m, w_br_a, w_out, norm2_g, w_router_grp, b_router_grp, w_router_exp, b_router_exp, w_exp_gate, w_exp_up, w_exp_down, final_norm_g):
    b, s, d = x.shape
    n_ctx = ctx.shape[1]
    assert w_mod.shape[0] == 1, "single-layer block"

    pad_rows = 8 - (b + 1)
    cvec = jnp.concatenate([c, c_ctx[None, :], jnp.zeros((pad_rows, d), F32)], axis=0)
    mods = _adaln(cvec, w_mod[0], b_mod[0])
    g1, sh2, sc2, g2 = [mods[:b, i * d:(i + 1) * d].reshape(b, 1, d) for i in range(2, 6)]
    sh1_all = mods[:, 0:d].reshape(-1, 1, d)
    sc1_all = mods[:, d:2 * d].reshape(-1, 1, d)

    t_lat = b * s
    h_all = _norm_mod(x.reshape(t_lat, d), ctx.reshape(b * n_ctx, d), norm1_g[0], sh1_all, sc1_all, s, b)
    w_in2d = w_in[0]
    p_all = _proj(h_all, w_in2d)
    b_gate = jnp.pad(mlstm_gate_b[0].astype(F32).reshape(1, GATE_W), ((0, 0), (0, LANES - GATE_W)))
    g_all = _gate_proj(h_all, w_in2d, b_gate)

    g_lat = g_all[:t_lat, :GATE_W].reshape(b, s // M_CHUNK, M_CHUNK, 4, M_HEADS)
    g_ctx = g_all[t_lat:, :GATE_W].reshape(b, n_ctx // M_CHUNK, M_CHUNK, 4, M_HEADS)
    gates = jnp.concatenate([g_ctx, g_lat], axis=1)
    gcol = jnp.transpose(gates, (0, 4, 1, 2, 3))
    grow = jnp.transpose(gates, (0, 4, 1, 3, 2))

    hm = _mlstm(p_all, b, s, n_ctx, gcol, grow, mlstm_norm_g[0])
    cos_t, sin_s = _rope_tables(s)
    att = _attention(p_all, b, s, n_ctx, attn_sink[0], cos_t, sin_s)

    merged = _merge(hm, att, p_all, w_br_m[0].astype(BF16), w_br_a[0].astype(BF16))

    w_r = jnp.concatenate([w_router_grp[0], w_router_exp[0]], axis=1)
    n_r = w_r.shape[1]
    w_r = jnp.pad(w_r, ((0, 0), (0, LANES - n_r))).astype(BF16)
    b_r = jnp.pad(jnp.concatenate([b_router_grp[0], b_router_exp[0]]), (0, LANES - n_r)).reshape(1, LANES)
    x1, h2, r_idx, r_w = _outproj(merged.reshape(b, s, d), x, w_out[0].astype(BF16), g1, norm2_g[0],
                                  sh2, sc2, w_r, b_r)

    t_tok = b * s
    eid = r_idx.reshape(t_tok, LANES)[:, :TOP_K]
    w_top = r_w[:, :, :TOP_K]
    n_super = (t_tok * TOP_K) // MOE_SUPER + N_EXPERTS
    sorted_tok, dest, sb_expert, sb_nsub, sb_base, n_used = _dispatch_tables(eid, n_super)
    ybuf = _moe_experts(h2.reshape(t_tok, d), sorted_tok, sb_expert, sb_nsub, sb_base, n_used, n_super,
                        w_exp_gate, w_exp_up, w_exp_down)
    ya = ybuf.at[dest[:, 0]].get(mode="promise_in_bounds").reshape(b, s, d)
    yb = ybuf.at[dest[:, 1]].get(mode="promise_in_bounds").reshape(b, s, d)

    return _final(x1, ya, yb, w_top, g2, final_norm_g)
```

```python
import functools

import jax
import jax.numpy as jnp
from jax import lax
from jax.experimental import pallas as pl
from jax.experimental.pallas import tpu as pltpu

F32 = jnp.float32
BF16 = jnp.bfloat16

D_MODEL = 2048
EPS = 1e-6
GRID_W = 64

M_HEADS = 8
M_DQK = 128
M_DV = 256
M_CHUNK = 256
assert M_DQK == 128, "mLSTM state rows are laid out one key dim per lane"

A_HEADS = 16
A_KV_HEADS = 4
A_GROUP = A_HEADS // A_KV_HEADS
A_DH = 128
A_BLOCK = 128
WINDOW = 128
ROPE_HALF = A_DH // 4
ROPE_BASE = 10000.0
assert WINDOW == A_BLOCK, "band masks assume the window equals the block size"

N_GROUPS = 8
EXPERTS_PER_GROUP = 8
N_EXPERTS = N_GROUPS * EXPERTS_PER_GROUP
TOP_K = 2
D_EXPERT = 1024

M_QK_W = M_HEADS * M_DQK
M_V_W = M_HEADS * M_DV
A_Q_W = A_HEADS * A_DH
A_KV_W = A_KV_HEADS * A_DH
GATE_W = 4 * M_HEADS

OFF_QM = 0
OFF_KM = OFF_QM + M_QK_W
OFF_VM = OFF_KM + M_QK_W
OFF_OM = OFF_VM + M_V_W
OFF_GM = OFF_OM + M_V_W
OFF_QA = OFF_GM + GATE_W
OFF_KA = OFF_QA + A_Q_W
OFF_VA = OFF_KA + A_KV_W
OFF_GBR = OFF_VA + A_KV_W
IN_WIDTH = OFF_GBR + 2 * D_MODEL

P_QM = 0
P_KM = P_QM + M_QK_W
P_VM = P_KM + M_QK_W
P_OM = P_VM + M_V_W
P_QA = P_OM + M_V_W
P_KA = P_QA + A_Q_W
P_VA = P_KA + A_KV_W
P_GBM = P_VA + A_KV_W
P_GBA = P_GBM + D_MODEL
P_WIDTH = P_GBA + D_MODEL

LANES = 128
NORM_ROWS = 512
PROJ_TM = 1024
PROJ_TN = 1024
MOE_SUB = 128
MOE_SUPER = 512
GATHER_UNROLL = 8
VMEM_LIMIT = 56 * 1024 * 1024
MOE_VMEM_LIMIT = 60 * 1024 * 1024


def _cparams(sem, vmem=VMEM_LIMIT):
    return pltpu.CompilerParams(dimension_semantics=sem, vmem_limit_bytes=vmem)


def _adaln_kernel(c_ref, w_ref, b_ref, o_ref):
    c = c_ref[...]
    a = (c * jax.nn.sigmoid(c)).astype(BF16)
    o_ref[...] = jnp.dot(a, w_ref[...].astype(BF16), preferred_element_type=F32) + b_ref[...]


def _adaln(cvec, w_mod, b_mod):
    rows, d = cvec.shape
    n = w_mod.shape[1]
    tn = 1024
    return pl.pallas_call(
        _adaln_kernel,
        grid=(n // tn,),
        in_specs=[pl.BlockSpec((rows, d), lambda j: (0, 0)),
                  pl.BlockSpec((d, tn), lambda j: (0, j)),
                  pl.BlockSpec((1, tn), lambda j: (0, j))],
        out_specs=pl.BlockSpec((rows, tn), lambda j: (0, j)),
        out_shape=jax.ShapeDtypeStruct((rows, n), F32),
        compiler_params=_cparams(("parallel",)),
        name="adaln",
    )(cvec, w_mod, b_mod.reshape(1, n))


def _norm_mod_kernel(x_ref, c_ref, g_ref, sh_ref, sc_ref, o_ref, *, n_lat_tiles):
    xv = jnp.where(pl.program_id(0) < n_lat_tiles, x_ref[...], c_ref[...])
    y = xv * lax.rsqrt(jnp.mean(xv * xv, axis=-1, keepdims=True) + EPS) * g_ref[...]
    o_ref[...] = (y * (1.0 + sc_ref[0]) + sh_ref[0]).astype(o_ref.dtype)


def _norm_mod(x2d, ctx2d, g, shift, scale, rows_per_batch, ctx_mod_row):
    t_lat, d = x2d.shape
    t_ctx = ctx2d.shape[0]
    tm = NORM_ROWS
    n_lat, n_ctx = t_lat // tm, t_ctx // tm
    tiles_per_batch = rows_per_batch // tm
    mod_map = lambda i: (jnp.where(i < n_lat, i // tiles_per_batch, ctx_mod_row), 0, 0)
    return pl.pallas_call(
        functools.partial(_norm_mod_kernel, n_lat_tiles=n_lat),
        grid=(n_lat + n_ctx,),
        in_specs=[pl.BlockSpec((tm, d), lambda i: (jnp.minimum(i, n_lat - 1), 0)),
                  pl.BlockSpec((tm, d), lambda i: (jnp.maximum(i - n_lat, 0), 0)),
                  pl.BlockSpec((1, d), lambda i: (0, 0)),
                  pl.BlockSpec((1, 1, d), mod_map),
                  pl.BlockSpec((1, 1, d), mod_map)],
        out_specs=pl.BlockSpec((tm, d), lambda i: (i, 0)),
        out_shape=jax.ShapeDtypeStruct((t_lat + t_ctx, d), BF16),
        compiler_params=_cparams(("parallel",)),
        name="norm_mod",
    )(x2d, ctx2d, g.reshape(1, d), shift, scale)


def _proj_kernel(h_ref, wa_ref, wb_ref, o_ref, w_bf, *, first_shifted):
    j = pl.program_id(0)
    tn = w_bf.shape[0]

    @pl.when(pl.program_id(1) == 0)
    def _():
        @pl.when(j < first_shifted)
        def _():
            w_bf[...] = wa_ref[...].astype(BF16)

        @pl.when(j >= first_shifted)
        def _():
            w_bf[0:tn - GATE_W, :] = wa_ref[GATE_W:tn, :].astype(BF16)
            w_bf[tn - GATE_W:tn, :] = wb_ref[...].astype(BF16)

    o_ref[...] = lax.dot_general(h_ref[...], w_bf[...], (((1,), (1,)), ((), ())),
                                 preferred_element_type=F32).astype(o_ref.dtype)


def _proj(h_all, w_in_t):
    t, d = h_all.shape
    tm, tn = PROJ_TM, PROJ_TN
    assert OFF_GM % tn == 0 and P_WIDTH % tn == 0 and t % tm == 0 and tn % GATE_W == 0
    return pl.pallas_call(
        functools.partial(_proj_kernel, first_shifted=OFF_GM // tn),
        grid=(P_WIDTH // tn, t // tm),
        in_specs=[pl.BlockSpec((tm, d), lambda j, i: (i, 0)),
                  pl.BlockSpec((tn, d), lambda j, i: (j, 0)),
                  pl.BlockSpec((GATE_W, d), lambda j, i: ((j + 1) * (tn // GATE_W), 0))],
        out_specs=pl.BlockSpec((tm, tn), lambda j, i: (i, j)),
        out_shape=jax.ShapeDtypeStruct((t, P_WIDTH), BF16),
        scratch_shapes=[pltpu.VMEM((tn, d), BF16)],
        compiler_params=_cparams(("parallel", "arbitrary")),
        name="in_proj",
    )(h_all, w_in_t, w_in_t)


def _log_sigmoid(x):
    return jnp.minimum(x, 0.0) - jnp.log(1.0 + jnp.exp(-jnp.abs(x)))


def _gate_proj_kernel(h_ref, w_ref, b_ref, o_ref):
    wr = lax.broadcasted_iota(jnp.int32, w_ref.shape, 0)
    w = jnp.where(wr < GATE_W, w_ref[...], 0.0).astype(BF16)
    g = lax.dot_general(h_ref[...], w, (((1,), (1,)), ((), ())), preferred_element_type=F32) + b_ref[...]
    lf = _log_sigmoid(g)
    L = g.shape[0]
    row = lax.broadcasted_iota(jnp.int32, (L, L), 0)
    col = lax.broadcasted_iota(jnp.int32, (L, L), 1)
    hi = lax.Precision.HIGHEST
    cum_f = jnp.dot((col <= row).astype(F32), lf, precision=hi, preferred_element_type=F32)
    cum_b = jnp.dot((col >= row).astype(F32), lf, precision=hi, preferred_element_type=F32)
    lane = lax.broadcasted_iota(jnp.int32, g.shape, 1)
    is_ff = (lane >= M_HEADS) & (lane < 2 * M_HEADS)
    is_fb = (lane >= 3 * M_HEADS) & (lane < 4 * M_HEADS)
    o_ref[...] = jnp.where(is_ff, cum_f, jnp.where(is_fb, cum_b, g))


def _gate_proj(h_all, w_in_t, b_gate):
    t, d = h_all.shape
    L = M_CHUNK
    assert OFF_GM % LANES == 0
    return pl.pallas_call(
        _gate_proj_kernel,
        grid=(t // L,),
        in_specs=[pl.BlockSpec((L, d), lambda i: (i, 0)),
                  pl.BlockSpec((LANES, d), lambda i: (OFF_GM // LANES, 0)),
                  pl.BlockSpec((1, LANES), lambda i: (0, 0))],
        out_specs=pl.BlockSpec((L, LANES), lambda i: (i, 0)),
        out_shape=jax.ShapeDtypeStruct((t, LANES), F32),
        compiler_params=_cparams(("parallel",)),
        name="gate_proj",
    )(h_all, w_in_t, b_gate)


def _mlstm_chunk(direction, qv, kv, ktv, vv, gcol, grow, c_ref, n_ref, m_ref, valid):
    gi = 2 * direction
    igr = grow[gi:gi + 1, :]
    b_r = grow[gi + 1:gi + 2, :]
    L = b_r.shape[1]
    igc = jnp.broadcast_to(gcol[:, gi:gi + 1], (L, LANES))
    b_c = jnp.broadcast_to(gcol[:, gi + 1:gi + 2], (L, LANES))
    wide = lambda a, n: jnp.concatenate([a] * (n // LANES), axis=1)
    b_end = b_r[:, L - 1:L] if direction == 0 else b_r[:, 0:1]
    kscale = M_DQK ** -0.5

    c_st = c_ref[direction]
    n_st = n_ref[direction]
    m_st = m_ref[direction]

    h = None
    if qv is not None:
        dm = jnp.where(valid, wide(b_c, L) - b_r + igr, -jnp.inf)
        inter = b_c + m_st
        m_t = jnp.maximum(inter, jnp.max(dm, axis=1, keepdims=True))
        w_inter = jnp.exp(inter - m_t)
        sc = lax.dot_general(qv, kv, (((1,), (1,)), ((), ())), preferred_element_type=F32) * kscale
        s = sc * jnp.exp(dm - wide(m_t, L))
        num = (wide(w_inter, M_DV) * jnp.dot(qv, c_st.astype(BF16), preferred_element_type=F32)
               + jnp.dot(s.astype(BF16), vv, preferred_element_type=F32))
        qn = (w_inter * jnp.sum(qv.astype(F32) * n_st, axis=1, keepdims=True)
              + jnp.sum(s, axis=1, keepdims=True))
        h = num * wide(1.0 / jnp.maximum(jnp.abs(qn), jnp.exp(-m_t)), M_DV)

    dec = b_end - b_c + igc
    m_new = jnp.maximum(b_end + m_st, jnp.max(dec, axis=0, keepdims=True))
    w_s = jnp.exp(dec - m_new) * kscale
    w_c = jnp.exp(b_end + m_st - m_new)
    wv = (wide(w_s, M_DV) * vv.astype(F32)).astype(BF16)
    c_ref[direction] = wide(w_c, M_DV) * c_st + jnp.dot(ktv, wv, preferred_element_type=F32)
    n_ref[direction] = w_c * n_st + jnp.sum(w_s * kv.astype(F32), axis=0, keepdims=True)
    m_ref[direction] = m_new
    return h


def _mlstm_kernel(q_ref, k_ref, kt_ref, v_ref, o_ref, kc_ref, ktc_ref, vc_ref, gcol_ref, grow_ref, ng_ref,
                  out_ref, c_ref, n_ref, m_ref, hf_ref, hb_ref, *, n_ctx_chunks, n_lat_chunks):
    L = M_CHUNK
    row = lax.broadcasted_iota(jnp.int32, (L, L), 0)
    col = lax.broadcasted_iota(jnp.int32, (L, L), 1)
    valid = (col <= row, col >= row)

    c_ref[...] = jnp.zeros_like(c_ref)
    n_ref[...] = jnp.zeros_like(n_ref)
    m_ref[...] = jnp.zeros_like(m_ref)

    for j in range(n_ctx_chunks):
        for direction, jj in ((0, j), (1, n_ctx_chunks - 1 - j)):
            _mlstm_chunk(direction, None, kc_ref[jj * L:(jj + 1) * L, :], ktc_ref[0, 0, jj],
                         vc_ref[jj * L:(jj + 1) * L, :], gcol_ref[0, 0, jj], grow_ref[0, 0, jj],
                         c_ref, n_ref, m_ref, valid[direction])

    def body(j, carry):
        for direction, jj, h_ref in ((0, j, hf_ref), (1, n_lat_chunks - 1 - j, hb_ref)):
            r0 = pl.multiple_of(jj * L, L)
            h = _mlstm_chunk(direction, q_ref[pl.ds(r0, L), :], k_ref[pl.ds(r0, L), :],
                             kt_ref[0, 0, jj], v_ref[pl.ds(r0, L), :],
                             gcol_ref[0, 0, n_ctx_chunks + jj], grow_ref[0, 0, n_ctx_chunks + jj],
                             c_ref, n_ref, m_ref, valid[direction])
            h_ref[pl.ds(r0, L), :] = h
        return carry

    lax.fori_loop(0, n_lat_chunks, body, 0)

    def finish(j, carry):
        r0 = pl.multiple_of(j * L, L)
        hh = hf_ref[pl.ds(r0, L), :] + hb_ref[pl.ds(r0, L), :]
        hh = hh * lax.rsqrt(jnp.mean(hh * hh, axis=-1, keepdims=True) + EPS)
        og = jax.nn.sigmoid(o_ref[pl.ds(r0, L), :].astype(F32))
        out_ref[pl.ds(r0, L), :] = (hh * ng_ref[...] * og).astype(out_ref.dtype)
        return carry

    lax.fori_loop(0, n_lat_chunks, finish, 0)


def _chunked_kt(k2d, b, n_chunks):
    k = k2d.reshape(b, n_chunks, M_CHUNK, M_HEADS, M_DQK)
    return jnp.transpose(k, (0, 3, 1, 4, 2))


def _mlstm(p_all, b, s, n_ctx, gcol, grow, norm_g):
    L = M_CHUNK
    nck, nlk = n_ctx // L, s // L
    t_lat = b * s
    cb = t_lat // n_ctx
    kt_lat = _chunked_kt(p_all[:t_lat, P_KM:P_KM + M_QK_W], b, nlk)
    kt_ctx = _chunked_kt(p_all[t_lat:, P_KM:P_KM + M_QK_W], b, nck)
    kern = functools.partial(_mlstm_kernel, n_ctx_chunks=nck, n_lat_chunks=nlk)
    return pl.pallas_call(
        kern,
        grid=(b, M_HEADS),
        in_specs=[pl.BlockSpec((s, M_DQK), lambda bi, h: (bi, P_QM // M_DQK + h)),
                  pl.BlockSpec((s, M_DQK), lambda bi, h: (bi, P_KM // M_DQK + h)),
                  pl.BlockSpec((1, 1, nlk, M_DQK, L), lambda bi, h: (bi, h, 0, 0, 0)),
                  pl.BlockSpec((s, M_DV), lambda bi, h: (bi, P_VM // M_DV + h)),
                  pl.BlockSpec((s, M_DV), lambda bi, h: (bi, P_OM // M_DV + h)),
                  pl.BlockSpec((n_ctx, M_DQK), lambda bi, h: (cb + bi, P_KM // M_DQK + h)),
                  pl.BlockSpec((1, 1, nck, M_DQK, L), lambda bi, h: (bi, h, 0, 0, 0)),
                  pl.BlockSpec((n_ctx, M_DV), lambda bi, h: (cb + bi, P_VM // M_DV + h)),
                  pl.BlockSpec((1, 1, nck + nlk, L, 4), lambda bi, h: (bi, h, 0, 0, 0)),
                  pl.BlockSpec((1, 1, nck + nlk, 4, L), lambda bi, h: (bi, h, 0, 0, 0)),
                  pl.BlockSpec((1, M_DV), lambda bi, h: (0, h))],
        out_specs=pl.BlockSpec((s, M_DV), lambda bi, h: (bi, h)),
        out_shape=jax.ShapeDtypeStruct((t_lat, M_V_W), BF16),
        scratch_shapes=[pltpu.VMEM((2, M_DQK, M_DV), F32),
                        pltpu.VMEM((2, 1, M_DQK), F32),
                        pltpu.VMEM((2, 1, LANES), F32),
                        pltpu.VMEM((s, M_DV), F32),
                        pltpu.VMEM((s, M_DV), F32)],
        compiler_params=_cparams(("parallel", "parallel")),
        name="mlstm",
    )(p_all, p_all, kt_lat, p_all, p_all, p_all, kt_ctx, p_all, gcol, grow, norm_g.reshape(1, M_V_W))


def _rope(t, cos_t, sin_s, lane_lo):
    swapped = jnp.where(lane_lo, pltpu.roll(t, A_DH - ROPE_HALF, 1), pltpu.roll(t, ROPE_HALF, 1))
    return t * cos_t + swapped * sin_s


def _attn_kernel(sink_ref, q_ref, k_ref, v_ref, kc_ref, vc_ref, cos_ref, sin_ref, out_ref, kpad, vpad):
    s = k_ref.shape[0]
    blk = A_BLOCK
    hk = pl.program_id(1)
    scale = A_DH ** -0.5
    lane = lax.broadcasted_iota(jnp.int32, (1, A_DH), 1)
    lane_lo = (lane % (2 * ROPE_HALF)) < ROPE_HALF

    zero = jnp.zeros((blk, A_DH), BF16)
    kpad[0:blk, :] = zero
    kpad[s + blk:s + 2 * blk, :] = zero
    vpad[0:blk, :] = zero
    vpad[s + blk:s + 2 * blk, :] = zero
    vpad[blk:s + blk, :] = v_ref[...]

    def rope_k(n, carry):
        r0 = pl.multiple_of(n * blk, blk)
        kf = k_ref[pl.ds(r0, blk), :].astype(F32)
        kr = _rope(kf, cos_ref[pl.ds(r0, blk), :], sin_ref[pl.ds(r0, blk), :], lane_lo)
        kpad[pl.ds(r0 + blk, blk), :] = kr.astype(BF16)
        return carry

    lax.fori_loop(0, s // blk, rope_k, 0)

    kc = kc_ref[...]
    vc = vc_ref[...]
    n_ctx = kc.shape[0]
    n_blocks = s // blk
    rr = lax.broadcasted_iota(jnp.int32, (blk, blk), 0)
    cc = lax.broadcasted_iota(jnp.int32, (blk, blk), 1)
    nt = (((1,), (1,)), ((), ()))
    log2e = 1.4426950408889634
    neg = -jnp.inf

    def body(n, carry):
        r0 = pl.multiple_of(n * blk, blk)
        cos_q = cos_ref[pl.ds(r0, blk), :]
        sin_q = sin_ref[pl.ds(r0, blk), :]
        kb = kpad[pl.ds(r0, 3 * blk), :]
        vb = vpad[pl.ds(r0, 3 * blk), :]
        bias_prev = jnp.where(cc >= rr, jnp.where(n > 0, 0.0, neg), neg)
        bias_next = jnp.where(cc <= rr, jnp.where(n < n_blocks - 1, 0.0, neg), neg)
        q_all = jnp.concatenate(
            [_rope(q_ref[pl.ds(r0, blk), g * A_DH:(g + 1) * A_DH].astype(F32), cos_q, sin_q,
                   lane_lo).astype(BF16) for g in range(A_GROUP)], axis=0)
        s_loc = lax.dot_general(q_all, kb, nt, preferred_element_type=F32)
        s_ctx = lax.dot_general(q_all, kc, nt, preferred_element_type=F32)
        p_parts, inv_den = [], []
        for g in range(A_GROUP):
            rows = slice(g * blk, (g + 1) * blk)
            tiles = ([s_loc[rows, 0:blk] + bias_prev, s_loc[rows, blk:2 * blk],
                      s_loc[rows, 2 * blk:3 * blk] + bias_next]
                     + [s_ctx[rows, c:c + blk] for c in range(0, n_ctx, blk)])
            tmax = functools.reduce(jnp.maximum, tiles)
            snk = sink_ref[hk * A_GROUP + g]
            m = jnp.maximum(jnp.max(tmax, axis=1, keepdims=True) * scale, snk)
            mb = m * log2e
            p = [jnp.exp2(t * (scale * log2e) - mb) for t in tiles]
            den = jnp.sum(functools.reduce(jnp.add, p), axis=1, keepdims=True) + jnp.exp(snk - m)
            inv_den.append(1.0 / den)
            p_parts.append(jnp.concatenate([t.astype(BF16) for t in p], axis=1))
        p_all = jnp.concatenate(p_parts, axis=0)
        o_all = (jnp.dot(p_all[:, :3 * blk], vb, preferred_element_type=F32)
                 + jnp.dot(p_all[:, 3 * blk:], vc, preferred_element_type=F32))
        for g in range(A_GROUP):
            o = o_all[g * blk:(g + 1) * blk, :] * inv_den[g]
            out_ref[pl.ds(r0, blk), g * A_DH:(g + 1) * A_DH] = o.astype(out_ref.dtype)
        return carry

    lax.fori_loop(0, n_blocks, body, 0)


def _attention(p_all, b, s, n_ctx, sink, cos_t, sin_s):
    gw = A_GROUP * A_DH
    cb = (b * s) // n_ctx
    return pl.pallas_call(
        _attn_kernel,
        grid=(b, A_KV_HEADS),
        in_specs=[pl.BlockSpec(memory_space=pltpu.SMEM),
                  pl.BlockSpec((s, gw), lambda bi, h: (bi, P_QA // gw + h)),
                  pl.BlockSpec((s, A_DH), lambda bi, h: (bi, P_KA // A_DH + h)),
                  pl.BlockSpec((s, A_DH), lambda bi, h: (bi, P_VA // A_DH + h)),
                  pl.BlockSpec((n_ctx, A_DH), lambda bi, h: (cb + bi, P_KA // A_DH + h)),
                  pl.BlockSpec((n_ctx, A_DH), lambda bi, h: (cb + bi, P_VA // A_DH + h)),
                  pl.BlockSpec((s, A_DH), lambda bi, h: (0, 0)),
                  pl.BlockSpec((s, A_DH), lambda bi, h: (0, 0))],
        out_specs=pl.BlockSpec((s, gw), lambda bi, h: (bi, h)),
        out_shape=jax.ShapeDtypeStruct((b * s, A_Q_W), BF16),
        scratch_shapes=[pltpu.VMEM((s + 2 * A_BLOCK, A_DH), BF16),
                        pltpu.VMEM((s + 2 * A_BLOCK, A_DH), BF16)],
        compiler_params=_cparams(("parallel", "parallel")),
        name="attention",
    )(sink, p_all, p_all, p_all, p_all, p_all, cos_t, sin_s)


def _rope_tables(n_tokens):
    rows = n_tokens // GRID_W
    row = jnp.repeat(jnp.arange(rows, dtype=F32), GRID_W)
    col = jnp.tile(jnp.arange(GRID_W, dtype=F32), rows)
    inv_freq = ROPE_BASE ** (-jnp.arange(ROPE_HALF, dtype=F32) / ROPE_HALF)
    ar = row[:, None] * inv_freq
    ac = col[:, None] * inv_freq
    cos_t = jnp.concatenate([jnp.cos(ar), jnp.cos(ar), jnp.cos(ac), jnp.cos(ac)], axis=1)
    sin_s = jnp.concatenate([-jnp.sin(ar), jnp.sin(ar), -jnp.sin(ac), jnp.sin(ac)], axis=1)
    return cos_t, sin_s


def _merge_kernel(hm_ref, at_ref, wm_ref, wa_ref, gm_ref, ga_ref, o_ref):
    bm = jnp.dot(hm_ref[...], wm_ref[...], preferred_element_type=F32)
    ba = jnp.dot(at_ref[...], wa_ref[...], preferred_element_type=F32)
    o = jax.nn.sigmoid(gm_ref[...].astype(F32)) * bm + jax.nn.sigmoid(ga_ref[...].astype(F32)) * ba
    o_ref[...] = o.astype(o_ref.dtype)


def _merge(hm, att, p2d, w_br_m, w_br_a):
    t, d = hm.shape
    tm, tn = 1024, 512
    return pl.pallas_call(
        _merge_kernel,
        grid=(t // tm, d // tn),
        in_specs=[pl.BlockSpec((tm, M_V_W), lambda i, j: (i, 0)),
                  pl.BlockSpec((tm, A_Q_W), lambda i, j: (i, 0)),
                  pl.BlockSpec((M_V_W, tn), lambda i, j: (0, j)),
                  pl.BlockSpec((A_Q_W, tn), lambda i, j: (0, j)),
                  pl.BlockSpec((tm, tn), lambda i, j: (i, P_GBM // tn + j)),
                  pl.BlockSpec((tm, tn), lambda i, j: (i, P_GBA // tn + j))],
        out_specs=pl.BlockSpec((tm, tn), lambda i, j: (i, j)),
        out_shape=jax.ShapeDtypeStruct((t, d), BF16),
        compiler_params=_cparams(("parallel", "arbitrary")),
        name="merge",
    )(hm, att, w_br_m, w_br_a, p2d, p2d)


def _outproj_kernel(mg_ref, x_ref, wo_ref, g1_ref, ng_ref, sh_ref, sc_ref, wr_ref, br_ref,
                    x1_ref, h2_ref, ri_ref, rw_ref):
    y = jnp.dot(mg_ref[0], wo_ref[...], preferred_element_type=F32)
    x1 = x_ref[0] + g1_ref[0] * y
    x1_ref[0] = x1
    hn = x1 * lax.rsqrt(jnp.mean(x1 * x1, axis=-1, keepdims=True) + EPS) * ng_ref[...]
    h2 = hn * (1.0 + sc_ref[0]) + sh_ref[0]
    h2_ref[0] = h2

    logits = jnp.dot(h2.astype(BF16), wr_ref[...], preferred_element_type=F32) + br_ref[...]
    lane = lax.broadcasted_iota(jnp.int32, logits.shape, 1)
    neg = -jnp.inf
    gl = jnp.where(lane < N_GROUPS, logits, neg)
    gmax = jnp.max(gl, axis=1, keepdims=True)
    gsel = jnp.min(jnp.where(gl == gmax, lane, LANES), axis=1, keepdims=True)
    p_sel = 1.0 / jnp.sum(jnp.exp(gl - gmax), axis=1, keepdims=True)
    lo = N_GROUPS + gsel * EXPERTS_PER_GROUP
    el = jnp.where((lane >= lo) & (lane < lo + EXPERTS_PER_GROUP), logits, neg)
    v1 = jnp.max(el, axis=1, keepdims=True)
    i1 = jnp.min(jnp.where(el == v1, lane, LANES), axis=1, keepdims=True)
    el2 = jnp.where(lane == i1, neg, el)
    v2 = jnp.max(el2, axis=1, keepdims=True)
    i2 = jnp.min(jnp.where(el2 == v2, lane, LANES), axis=1, keepdims=True)
    e2 = jnp.exp(v2 - v1)
    w1 = p_sel / (1.0 + e2)
    w2 = p_sel * e2 / (1.0 + e2)
    ri_ref[0] = jnp.where(lane == 0, i1 - N_GROUPS, jnp.where(lane == 1, i2 - N_GROUPS, 0))
    rw_ref[0] = jnp.where(lane == 0, w1, jnp.where(lane == 1, w2, 0.0))


def _outproj(merged, x, w_out, g1, norm_g, shift, scale, w_r, b_r):
    b, s, d = x.shape
    tm = 512
    row = lambda bi, i: (bi, i, 0)
    per_b = lambda bi, i: (bi, 0, 0)
    fixed = lambda bi, i: (0, 0)
    return pl.pallas_call(
        _outproj_kernel,
        grid=(b, s // tm),
        in_specs=[pl.BlockSpec((1, tm, d), row),
                  pl.BlockSpec((1, tm, d), row),
                  pl.BlockSpec((d, d), fixed),
                  pl.BlockSpec((1, 1, d), per_b),
                  pl.BlockSpec((1, d), fixed),
                  pl.BlockSpec((1, 1, d), per_b),
                  pl.BlockSpec((1, 1, d), per_b),
                  pl.BlockSpec((d, LANES), fixed),
                  pl.BlockSpec((1, LANES), fixed)],
        out_specs=[pl.BlockSpec((1, tm, d), row),
                   pl.BlockSpec((1, tm, d), row),
                   pl.BlockSpec((1, tm, LANES), row),
                   pl.BlockSpec((1, tm, LANES), row)],
        out_shape=[jax.ShapeDtypeStruct((b, s, d), F32),
                   jax.ShapeDtypeStruct((b, s, d), F32),
                   jax.ShapeDtypeStruct((b, s, LANES), jnp.int32),
                   jax.ShapeDtypeStruct((b, s, LANES), F32)],
        compiler_params=_cparams(("parallel", "parallel")),
        name="outproj_router",
    )(merged, x, w_out, g1, norm_g.reshape(1, d), shift, scale, w_r, b_r)


def _moe_kernel(sbe_ref, nsub_ref, base_ref, nused_ref, tok_ref, h_hbm, wg_hbm, wu_hbm, wd_hbm, y_hbm,
                xbuf, xb, ring_gu, ring_d, wb_gu, wb_d, hgu, act, ostage, sem, wsem, osem, *, n_super):
    nused = nused_ref[0]
    n_assign = tok_ref.shape[0]
    d = h_hbm.shape[1]
    kh = d // 2
    fh = D_EXPERT // 2

    def w_copy(c, sb):
        e = sbe_ref[sb]
        if c < 4:
            w_hbm = wg_hbm if c % 2 == 0 else wu_hbm
            return pltpu.make_async_copy(w_hbm.at[0, e, pl.ds((c // 2) * kh, kh), :], ring_gu.at[c],
                                         wsem.at[c])
        return pltpu.make_async_copy(wd_hbm.at[0, e, pl.ds((c - 4) * fh, fh), :], ring_d.at[c - 4],
                                     wsem.at[c])

    def out_copy(sb, p):
        return pltpu.make_async_copy(ostage.at[p], y_hbm.at[pl.ds(sb * MOE_SUPER, MOE_SUPER)], osem.at[p])

    def start_gather(sb, slot):
        base = base_ref[sb]

        def issue(i, carry):
            for u in range(GATHER_UNROLL):
                r = i * GATHER_UNROLL + u
                tok = tok_ref[jnp.minimum(base + r, n_assign - 1)]
                pltpu.make_async_copy(h_hbm.at[pl.ds(tok, 1)], xbuf.at[slot, pl.ds(r, 1)],
                                      sem.at[slot]).start()
            return carry

        lax.fori_loop(0, nsub_ref[sb] * (MOE_SUB // GATHER_UNROLL), issue, 0)

    def wait_gather(sb, slot):
        def wait_sub(i, carry):
            pltpu.make_async_copy(h_hbm.at[pl.ds(0, MOE_SUB)], xbuf.at[slot, pl.ds(0, MOE_SUB)],
                                  sem.at[slot]).wait()
            return carry

        lax.fori_loop(0, nsub_ref[sb], wait_sub, 0)

    def job_compute(c, n, p):
        def sub(i, carry):
            r0 = pl.multiple_of(i * MOE_SUB, MOE_SUB)
            rows = pl.ds(r0, MOE_SUB)
            if c < 4:
                part = jnp.dot(xb[rows, (c // 2) * kh:(c // 2 + 1) * kh], wb_gu[...],
                               preferred_element_type=F32)
                if c < 2:
                    hgu[c, rows, :] = part
                elif c == 2:
                    hgu[0, rows, :] += part
                else:
                    g = hgu[0, rows, :]
                    act[rows, :] = (g * jax.nn.sigmoid(g) * (hgu[1, rows, :] + part)).astype(BF16)
            else:
                part = jnp.dot(act[rows, (c - 4) * fh:(c - 3) * fh], wb_d[...],
                               preferred_element_type=F32)
                if c == 4:
                    hgu[0, rows, :] = part[:, :kh]
                    hgu[1, rows, :] = part[:, kh:]
                else:
                    y = part + jnp.concatenate([hgu[0, rows, :], hgu[1, rows, :]], axis=1)
                    ostage[p, rows, :] = y.astype(ostage.dtype)
            return carry

        lax.fori_loop(0, n, sub, 0)

    ostage[...] = jnp.zeros_like(ostage)
    start_gather(0, 0)
    for c in range(6):
        w_copy(c, 0).start()

    def superblock(s, carry):
        slot = s % 2
        n = nsub_ref[s]
        wait_gather(s, slot)

        @pl.when(s + 1 < nused)
        def _():
            start_gather(s + 1, 1 - slot)

        def cast_x(i, c2):
            rows = pl.ds(pl.multiple_of(i * MOE_SUB, MOE_SUB), MOE_SUB)
            xb[rows, :] = xbuf[slot, rows, :].astype(BF16)
            return c2

        lax.fori_loop(0, n, cast_x, 0)

        for c in range(6):
            w_copy(c, s).wait()
            if c < 4:
                wb_gu[...] = ring_gu[c].astype(BF16)
            else:
                wb_d[...] = ring_d[c - 4].astype(BF16)

            @pl.when(s + 1 < nused)
            def _():
                w_copy(c, s + 1).start()

            if c == 5:
                @pl.when(s >= 2)
                def _():
                    out_copy(s - 2, slot).wait()

            job_compute(c, n, slot)

        out_copy(s, slot).start()
        return carry

    lax.fori_loop(0, nused, superblock, 0)

    @pl.when(nused >= 2)
    def _():
        out_copy(nused - 2, nused % 2).wait()

    out_copy(nused - 1, (nused - 1) % 2).wait()

    ostage[0] = jnp.zeros(ostage.shape[1:], ostage.dtype)

    def zero_start(sb, carry):
        out_copy(sb, 0).start()
        return carry

    def zero_wait(sb, carry):
        out_copy(sb, 0).wait()
        return carry

    lax.fori_loop(nused, n_super, zero_start, 0)
    lax.fori_loop(nused, n_super, zero_wait, 0)


def _moe_experts(h2, sorted_tok, sb_expert, sb_nsub, sb_base, n_used, n_super, w_g, w_u, w_d):
    _, d = h2.shape
    assert d % 2 == 0 and D_EXPERT % 2 == 0 and d // 2 == D_EXPERT
    any_spec = pl.BlockSpec(memory_space=pl.ANY)
    grid_spec = pltpu.PrefetchScalarGridSpec(
        num_scalar_prefetch=5,
        grid=(1,),
        in_specs=[any_spec, any_spec, any_spec, any_spec],
        out_specs=any_spec,
        scratch_shapes=[pltpu.VMEM((2, MOE_SUPER, d), F32),
                        pltpu.VMEM((MOE_SUPER, d), BF16),
                        pltpu.VMEM((4, d // 2, D_EXPERT), F32),
                        pltpu.VMEM((2, D_EXPERT // 2, d), F32),
                        pltpu.VMEM((d // 2, D_EXPERT), BF16),
                        pltpu.VMEM((D_EXPERT // 2, d), BF16),
                        pltpu.VMEM((2, MOE_SUPER, D_EXPERT), F32),
                        pltpu.VMEM((MOE_SUPER, D_EXPERT), BF16),
                        pltpu.VMEM((2, MOE_SUPER, d), BF16),
                        pltpu.SemaphoreType.DMA((2,)),
                        pltpu.SemaphoreType.DMA((6,)),
                        pltpu.SemaphoreType.DMA((2,))],
    )
    return pl.pallas_call(
        functools.partial(_moe_kernel, n_super=n_super),
        grid_spec=grid_spec,
        out_shape=jax.ShapeDtypeStruct((n_super * MOE_SUPER, d), BF16),
        compiler_params=_cparams(("arbitrary",), vmem=MOE_VMEM_LIMIT),
        name="moe_experts",
    )(sb_expert, sb_nsub, sb_base, n_used, sorted_tok, h2, w_g, w_u, w_d)


def _dispatch_tables(eid, n_super):
    flat = eid.reshape(-1)
    n_assign = flat.shape[0]
    order = jnp.argsort(flat).astype(jnp.int32)
    sorted_tok = order // TOP_K
    counts = jnp.bincount(flat, length=N_EXPERTS)
    nsup = (counts + MOE_SUPER - 1) // MOE_SUPER
    sup_end = jnp.cumsum(nsup)
    sup_start = sup_end - nsup
    start = jnp.cumsum(counts) - counts
    off = sup_start * MOE_SUPER - start
    d_off = jnp.diff(off, prepend=0)
    pos = jnp.arange(n_assign)
    off_sorted = jnp.sum(jnp.where(pos[:, None] >= start[None, :], d_off[None, :], 0), axis=1)
    dest_sorted = (pos + off_sorted).astype(jnp.int32)
    _, dest = lax.sort((order, dest_sorted), num_keys=1)
    sidx = jnp.arange(n_super)
    sb_expert = jnp.minimum(jnp.sum(sup_end[None, :] <= sidx[:, None], axis=1), N_EXPERTS - 1)
    n_used = sup_end[-1]
    k = sidx - sup_start[sb_expert]
    rows_in = jnp.clip(counts[sb_expert] - k * MOE_SUPER, 0, MOE_SUPER)
    sb_nsub = jnp.where(sidx < n_used, (rows_in + MOE_SUB - 1) // MOE_SUB, 0)
    sb_base = start[sb_expert] + k * MOE_SUPER
    i32 = lambda a: a.astype(jnp.int32)
    return (sorted_tok, dest.reshape(-1, TOP_K), i32(sb_expert), i32(sb_nsub), i32(sb_base),
            i32(n_used).reshape(1))


def _final_kernel(x1_ref, ya_ref, yb_ref, w_ref, g2_ref, fg_ref, o_ref):
    w = w_ref[0]
    moe = w[:, 0:1] * ya_ref[0].astype(F32) + w[:, 1:2] * yb_ref[0].astype(F32)
    x2 = x1_ref[0] + g2_ref[0] * moe
    o_ref[0] = x2 * lax.rsqrt(jnp.mean(x2 * x2, axis=-1, keepdims=True) + EPS) * fg_ref[...]


def _final(x1, ya, yb, w_top, g2, final_g):
    b, s, d = x1.shape
    ts = 512
    row = lambda bi, i: (bi, i, 0)
    return pl.pallas_call(
        _final_kernel,
        grid=(b, s // ts),
        in_specs=[pl.BlockSpec((1, ts, d), row),
                  pl.BlockSpec((1, ts, d), row),
                  pl.BlockSpec((1, ts, d), row),
                  pl.BlockSpec((1, ts, TOP_K), row),
                  pl.BlockSpec((1, 1, d), lambda bi, i: (bi, 0, 0)),
                  pl.BlockSpec((1, d), lambda bi, i: (0, 0))],
        out_specs=pl.BlockSpec((1, ts, d), row),
        out_shape=jax.ShapeDtypeStruct((b, s, d), F32),
        compiler_params=_cparams(("parallel", "parallel")),
        name="final_norm",
    )(x1, ya, yb, w_top, g2, final_g.reshape(1, d))


def kernel(x, c, ctx, c_ctx, w_mod, b_mod, norm1_g, w_in, mlstm_gate_b, mlstm_norm_g, attn_sink, w_br_m, w_br_a, w_out, norm2_g, w_router_grp, b_router_grp, w_router_exp, b_router_exp, w_exp_gate, w_exp_up, w_exp_down, final_norm_g):
    b, s, d = x.shape
    n_ctx = ctx.shape[1]
    assert w_mod.shape[0] == 1, "single-layer block"

    pad_rows = 8 - (b + 1)
    cvec = jnp.concatenate([c, c_ctx[None, :], jnp.zeros((pad_rows, d), F32)], axis=0)
    mods = _adaln(cvec, w_mod[0], b_mod[0])
    g1, sh2, sc2, g2 = [mods[:b, i * d:(i + 1) * d].reshape(b, 1, d) for i in range(2, 6)]
    sh1_all = mods[:, 0:d].reshape(-1, 1, d)
    sc1_all = mods[:, d:2 * d].reshape(-1, 1, d)

    t_lat = b * s
    h_all = _norm_mod(x.reshape(t_lat, d), ctx.reshape(b * n_ctx, d), norm1_g[0], sh1_all, sc1_all, s, b)
    w_in_t = jnp.transpose(w_in[0])
    p_all = _proj(h_all, w_in_t)
    b_gate = jnp.pad(mlstm_gate_b[0].astype(F32).reshape(1, GATE_W), ((0, 0), (0, LANES - GATE_W)))
    g_all = _gate_proj(h_all, w_in_t, b_gate)

    g_lat = g_all[:t_lat, :GATE_W].reshape(b, s // M_CHUNK, M_CHUNK, 4, M_HEADS)
    g_ctx = g_all[t_lat:, :GATE_W].reshape(b, n_ctx // M_CHUNK, M_CHUNK, 4, M_HEADS)
    gates = jnp.concatenate([g_ctx, g_lat], axis=1)
    gcol = jnp.transpose(gates, (0, 4, 1, 2, 3))
    grow = jnp.transpose(gates, (0, 4, 1, 3, 2))

    hm = _mlstm(p_all, b, s, n_ctx, gcol, grow, mlstm_norm_g[0])
    cos_t, sin_s = _rope_tables(s)
    att = _attention(p_all, b, s, n_ctx, attn_sink[0], cos_t, sin_s)

    merged = _merge(hm, att, p_all, w_br_m[0].astype(BF16), w_br_a[0].astype(BF16))

    w_r = jnp.concatenate([w_router_grp[0], w_router_exp[0]], axis=1)
    n_r = w_r.shape[1]
    w_r = jnp.pad(w_r, ((0, 0), (0, LANES - n_r))).astype(BF16)
    b_r = jnp.pad(jnp.concatenate([b_router_grp[0], b_router_exp[0]]), (0, LANES - n_r)).reshape(1, LANES)
    x1, h2, r_idx, r_w = _outproj(merged.reshape(b, s, d), x, w_out[0].astype(BF16), g1, norm2_g[0],
                                  sh2, sc2, w_r, b_r)

    t_tok = b * s
    eid = r_idx.reshape(t_tok, LANES)[:, :TOP_K]
    w_top = r_w[:, :, :TOP_K]
    n_super = (t_tok * TOP_K) // MOE_SUPER + N_EXPERTS
    sorted_tok, dest, sb_expert, sb_nsub, sb_base, n_used = _dispatch_tables(eid, n_super)
    ybuf = _moe_experts(h2.reshape(t_tok, d), sorted_tok, sb_expert, sb_nsub, sb_base, n_used, n_super,
                        w_exp_gate, w_exp_up, w_exp_down)
    ya = ybuf.at[dest[:, 0]].get(mode="promise_in_bounds").reshape(b, s, d)
    yb = ybuf.at[dest[:, 1]].get(mode="promise_in_bounds").reshape(b, s, d)

    return _final(x1, ya, yb, w_top, g2, final_norm_g)
```

```python
import functools
import math

import jax
import jax.numpy as jnp
from jax import lax
from jax.experimental import pallas as pl
from jax.experimental.pallas import tpu as pltpu

F32 = jnp.float32
BF16 = jnp.bfloat16

D_MODEL = 2048
EPS = 1e-6
GRID_W = 64

M_HEADS = 8
M_DQK = 128
M_DV = 256
M_CHUNK = 256
MLSTM_ROWS = 256
assert M_DQK == 128, "mLSTM state rows are laid out one key dim per lane"

A_HEADS = 16
A_KV_HEADS = 4
A_GROUP = A_HEADS // A_KV_HEADS
A_DH = 128
A_BLOCK = 128
WINDOW = 128
ROPE_HALF = A_DH // 4
ROPE_BASE = 10000.0
assert WINDOW == A_BLOCK, "band masks assume the window equals the block size"

N_GROUPS = 8
EXPERTS_PER_GROUP = 8
N_EXPERTS = N_GROUPS * EXPERTS_PER_GROUP
TOP_K = 2
D_EXPERT = 1024

M_QK_W = M_HEADS * M_DQK
M_V_W = M_HEADS * M_DV
A_Q_W = A_HEADS * A_DH
A_KV_W = A_KV_HEADS * A_DH
GATE_W = 4 * M_HEADS

OFF_QM = 0
OFF_KM = OFF_QM + M_QK_W
OFF_VM = OFF_KM + M_QK_W
OFF_OM = OFF_VM + M_V_W
OFF_GM = OFF_OM + M_V_W
OFF_QA = OFF_GM + GATE_W
OFF_KA = OFF_QA + A_Q_W
OFF_VA = OFF_KA + A_KV_W
OFF_GBR = OFF_VA + A_KV_W
IN_WIDTH = OFF_GBR + 2 * D_MODEL

P_QM = 0
P_KM = P_QM + M_QK_W
P_VM = P_KM + M_QK_W
P_OM = P_VM + M_V_W
P_QA = P_OM + M_V_W
P_KA = P_QA + A_Q_W
P_VA = P_KA + A_KV_W
P_GBM = P_VA + A_KV_W
P_GBA = P_GBM + D_MODEL
P_WIDTH = P_GBA + D_MODEL

LANES = 128
NORM_ROWS = 512
PROJ_TM = 1024
PROJ_TN = 1024
MOE_SUB = 128
MOE_SUPER = 512
GATHER_UNROLL = 8
VMEM_LIMIT = 56 * 1024 * 1024
MOE_VMEM_LIMIT = 60 * 1024 * 1024


def _cparams(sem, vmem=VMEM_LIMIT):
    return pltpu.CompilerParams(dimension_semantics=sem, vmem_limit_bytes=vmem)


def _adaln_kernel(c_ref, w_ref, b_ref, o_ref):
    c = c_ref[...]
    a = (c * jax.nn.sigmoid(c)).astype(BF16)
    o_ref[...] = jnp.dot(a, w_ref[...].astype(BF16), preferred_element_type=F32) + b_ref[...]


def _adaln(cvec, w_mod, b_mod):
    rows, d = cvec.shape
    n = w_mod.shape[1]
    tn = 1024
    return pl.pallas_call(
        _adaln_kernel,
        grid=(n // tn,),
        in_specs=[pl.BlockSpec((rows, d), lambda j: (0, 0)),
                  pl.BlockSpec((d, tn), lambda j: (0, j)),
                  pl.BlockSpec((1, tn), lambda j: (0, j))],
        out_specs=pl.BlockSpec((rows, tn), lambda j: (0, j)),
        out_shape=jax.ShapeDtypeStruct((rows, n), F32),
        compiler_params=_cparams(("parallel",)),
        name="adaln",
    )(cvec, w_mod, b_mod.reshape(1, n))


def _norm_mod_kernel(x_ref, c_ref, g_ref, sh_ref, sc_ref, o_ref, *, n_lat_tiles):
    xv = jnp.where(pl.program_id(0) < n_lat_tiles, x_ref[...], c_ref[...])
    y = xv * lax.rsqrt(jnp.mean(xv * xv, axis=-1, keepdims=True) + EPS) * g_ref[...]
    o_ref[...] = (y * (1.0 + sc_ref[0]) + sh_ref[0]).astype(o_ref.dtype)


def _norm_mod(x2d, ctx2d, g, shift, scale, rows_per_batch, ctx_mod_row):
    t_lat, d = x2d.shape
    t_ctx = ctx2d.shape[0]
    tm = NORM_ROWS
    n_lat, n_ctx = t_lat // tm, t_ctx // tm
    tiles_per_batch = rows_per_batch // tm
    mod_map = lambda i: (jnp.where(i < n_lat, i // tiles_per_batch, ctx_mod_row), 0, 0)
    return pl.pallas_call(
        functools.partial(_norm_mod_kernel, n_lat_tiles=n_lat),
        grid=(n_lat + n_ctx,),
        in_specs=[pl.BlockSpec((tm, d), lambda i: (jnp.minimum(i, n_lat - 1), 0)),
                  pl.BlockSpec((tm, d), lambda i: (jnp.maximum(i - n_lat, 0), 0)),
                  pl.BlockSpec((1, d), lambda i: (0, 0)),
                  pl.BlockSpec((1, 1, d), mod_map),
                  pl.BlockSpec((1, 1, d), mod_map)],
        out_specs=pl.BlockSpec((tm, d), lambda i: (i, 0)),
        out_shape=jax.ShapeDtypeStruct((t_lat + t_ctx, d), BF16),
        compiler_params=_cparams(("parallel",)),
        name="norm_mod",
    )(x2d, ctx2d, g.reshape(1, d), shift, scale)


def _proj_kernel(h_ref, wa_ref, wb_ref, o_ref, w_bf, *, first_shifted, n_lat_tiles, ctx_col_tiles):
    j = pl.program_id(0)
    tn = w_bf.shape[0]

    @pl.when(pl.program_id(1) == 0)
    def _():
        @pl.when(j < first_shifted)
        def _():
            w_bf[...] = wa_ref[...].astype(BF16)

        @pl.when(j >= first_shifted)
        def _():
            w_bf[0:tn - GATE_W, :] = wa_ref[GATE_W:tn, :].astype(BF16)
            w_bf[tn - GATE_W:tn, :] = wb_ref[...].astype(BF16)

    needed = pl.program_id(1) < n_lat_tiles
    for lo, hi in ctx_col_tiles:
        needed = needed | ((j >= lo) & (j < hi))

    @pl.when(needed)
    def _():
        o_ref[...] = lax.dot_general(h_ref[...], w_bf[...], (((1,), (1,)), ((), ())),
                                     preferred_element_type=F32).astype(o_ref.dtype)

    @pl.when(jnp.logical_not(needed))
    def _():
        o_ref[...] = jnp.zeros_like(o_ref)


def _proj(h_all, w_in_t, t_lat):
    t, d = h_all.shape
    tm, tn = PROJ_TM, PROJ_TN
    assert OFF_GM % tn == 0 and P_WIDTH % tn == 0 and t % tm == 0 and tn % GATE_W == 0 and t_lat % tm == 0
    ctx_col_tiles = ((P_KM // tn, -(-P_OM // tn)), (P_KA // tn, -(-P_GBM // tn)))
    return pl.pallas_call(
        functools.partial(_proj_kernel, first_shifted=OFF_GM // tn, n_lat_tiles=t_lat // tm,
                          ctx_col_tiles=ctx_col_tiles),
        grid=(P_WIDTH // tn, t // tm),
        in_specs=[pl.BlockSpec((tm, d), lambda j, i: (i, 0)),
                  pl.BlockSpec((tn, d), lambda j, i: (j, 0)),
                  pl.BlockSpec((GATE_W, d), lambda j, i: ((j + 1) * (tn // GATE_W), 0))],
        out_specs=pl.BlockSpec((tm, tn), lambda j, i: (i, j)),
        out_shape=jax.ShapeDtypeStruct((t, P_WIDTH), BF16),
        scratch_shapes=[pltpu.VMEM((tn, d), BF16)],
        compiler_params=_cparams(("parallel", "arbitrary")),
        name="in_proj",
    )(h_all, w_in_t, w_in_t)


def _log_sigmoid(x):
    return jnp.minimum(x, 0.0) - jnp.log(1.0 + jnp.exp(-jnp.abs(x)))


def _gate_proj_kernel(h_ref, w_ref, b_ref, o_ref):
    wr = lax.broadcasted_iota(jnp.int32, w_ref.shape, 0)
    w = jnp.where(wr < GATE_W, w_ref[...], 0.0).astype(BF16)
    g = lax.dot_general(h_ref[...], w, (((1,), (1,)), ((), ())), preferred_element_type=F32) + b_ref[...]
    lf = _log_sigmoid(g)
    L = g.shape[0]
    row = lax.broadcasted_iota(jnp.int32, (L, L), 0)
    col = lax.broadcasted_iota(jnp.int32, (L, L), 1)
    hi = lax.Precision.HIGHEST
    cum_f = jnp.dot((col <= row).astype(F32), lf, precision=hi, preferred_element_type=F32)
    cum_b = jnp.dot((col >= row).astype(F32), lf, precision=hi, preferred_element_type=F32)
    lane = lax.broadcasted_iota(jnp.int32, g.shape, 1)
    is_ff = (lane >= M_HEADS) & (lane < 2 * M_HEADS)
    is_fb = (lane >= 3 * M_HEADS) & (lane < 4 * M_HEADS)
    o_ref[...] = jnp.where(is_ff, cum_f, jnp.where(is_fb, cum_b, g))


def _gate_proj(h_all, w_in_t, b_gate):
    t, d = h_all.shape
    L = M_CHUNK
    assert OFF_GM % LANES == 0
    return pl.pallas_call(
        _gate_proj_kernel,
        grid=(t // L,),
        in_specs=[pl.BlockSpec((L, d), lambda i: (i, 0)),
                  pl.BlockSpec((LANES, d), lambda i: (OFF_GM // LANES, 0)),
                  pl.BlockSpec((1, LANES), lambda i: (0, 0))],
        out_specs=pl.BlockSpec((L, LANES), lambda i: (i, 0)),
        out_shape=jax.ShapeDtypeStruct((t, LANES), F32),
        compiler_params=_cparams(("parallel",)),
        name="gate_proj",
    )(h_all, w_in_t, b_gate)


def _mlstm_chunk(direction, qv, kv, ktv, vv, gcol, grow, c_ref, n_ref, m_ref, valid):
    gi = 2 * direction
    igr = grow[gi:gi + 1, :]
    b_r = grow[gi + 1:gi + 2, :]
    L = b_r.shape[1]
    wide = lambda a, n: jnp.concatenate([a] * (n // LANES), axis=1)
    b_end = b_r[:, L - 1:L] if direction == 0 else b_r[:, 0:1]
    kscale = M_DQK ** -0.5

    c_st = c_ref[direction]
    n_st = n_ref[direction]
    m_st = m_ref[direction]

    h = None
    if qv is not None:
        log2e = 1.4426950408889634
        log2k = -0.5 * math.log2(M_DQK)
        row_term = (igr - b_r) * log2e + log2k
        c_bf = c_st.astype(BF16)
        parts = []
        for r in range(0, L, MLSTM_ROWS):
            rs = slice(r, r + MLSTM_ROWS)
            q_r = qv[rs]
            b_c = jnp.broadcast_to(gcol[rs, gi + 1:gi + 2], (MLSTM_ROWS, LANES))
            dm2 = jnp.where(valid[rs], wide(b_c * log2e, L) + row_term, -jnp.inf)
            inter = b_c + m_st
            m_t = jnp.maximum(inter, (jnp.max(dm2, axis=1, keepdims=True) - log2k) * (1.0 / log2e))
            w_inter = jnp.exp(inter - m_t)
            sc = lax.dot_general(q_r, kv, (((1,), (1,)), ((), ())), preferred_element_type=F32)
            s = sc * jnp.exp2(dm2 - wide(m_t * log2e, L))
            num = (wide(w_inter, M_DV) * jnp.dot(q_r, c_bf, preferred_element_type=F32)
                   + jnp.dot(s.astype(BF16), vv, preferred_element_type=F32))
            qn = (w_inter * jnp.sum(q_r.astype(F32) * n_st, axis=1, keepdims=True)
                  + jnp.sum(s, axis=1, keepdims=True))
            parts.append(num * wide(1.0 / jnp.maximum(jnp.abs(qn), jnp.exp(-m_t)), M_DV))
        h = jnp.concatenate(parts, axis=0)

    dec = b_end - b_r + igr
    m_new = jnp.maximum(b_end + m_st, jnp.max(dec, axis=1, keepdims=True))
    w_s = jnp.exp(dec - m_new[:, 0:1]) * kscale
    w_c = jnp.exp(b_end + m_st - m_new)
    wk = (ktv.astype(F32) * w_s).astype(BF16)
    c_ref[direction] = wide(w_c, M_DV) * c_st + jnp.dot(wk, vv, preferred_element_type=F32)
    n_ref[direction] = w_c * n_st + jnp.dot(w_s.astype(BF16), kv, preferred_element_type=F32)
    m_ref[direction] = m_new
    return h


def _mlstm_kernel(q_ref, k_ref, kt_ref, v_ref, o_ref, kc_ref, ktc_ref, vc_ref, gcol_ref, grow_ref, ng_ref,
                  out_ref, c_ref, n_ref, m_ref, hf_ref, hb_ref, *, n_ctx_chunks, n_lat_chunks):
    L = M_CHUNK
    row = lax.broadcasted_iota(jnp.int32, (L, L), 0)
    col = lax.broadcasted_iota(jnp.int32, (L, L), 1)
    valid = (col <= row, col >= row)

    c_ref[...] = jnp.zeros_like(c_ref)
    n_ref[...] = jnp.zeros_like(n_ref)
    m_ref[...] = jnp.zeros_like(m_ref)

    for j in range(n_ctx_chunks):
        for direction, jj in ((0, j), (1, n_ctx_chunks - 1 - j)):
            _mlstm_chunk(direction, None, kc_ref[jj * L:(jj + 1) * L, :], ktc_ref[0, 0, jj],
                         vc_ref[jj * L:(jj + 1) * L, :], gcol_ref[0, 0, jj], grow_ref[0, 0, jj],
                         c_ref, n_ref, m_ref, valid[direction])

    def body(j, carry):
        for direction, jj, h_ref in ((0, j, hf_ref), (1, n_lat_chunks - 1 - j, hb_ref)):
            r0 = pl.multiple_of(jj * L, L)
            h = _mlstm_chunk(direction, q_ref[pl.ds(r0, L), :], k_ref[pl.ds(r0, L), :],
                             kt_ref[0, 0, jj], v_ref[pl.ds(r0, L), :],
                             gcol_ref[0, 0, n_ctx_chunks + jj], grow_ref[0, 0, n_ctx_chunks + jj],
                             c_ref, n_ref, m_ref, valid[direction])
            h_ref[pl.ds(r0, L), :] = h
        return carry

    lax.fori_loop(0, n_lat_chunks, body, 0)

    def finish(j, carry):
        r0 = pl.multiple_of(j * L, L)
        hh = hf_ref[pl.ds(r0, L), :] + hb_ref[pl.ds(r0, L), :]
        hh = hh * lax.rsqrt(jnp.mean(hh * hh, axis=-1, keepdims=True) + EPS)
        og = jax.nn.sigmoid(o_ref[pl.ds(r0, L), :].astype(F32))
        out_ref[pl.ds(r0, L), :] = (hh * ng_ref[...] * og).astype(out_ref.dtype)
        return carry

    lax.fori_loop(0, n_lat_chunks, finish, 0)


def _chunked_kt(k2d, b, n_chunks):
    k = k2d.reshape(b, n_chunks, M_CHUNK, M_HEADS, M_DQK)
    return jnp.transpose(k, (0, 3, 1, 4, 2))


def _mlstm(p_all, b, s, n_ctx, gcol, grow, norm_g):
    L = M_CHUNK
    nck, nlk = n_ctx // L, s // L
    t_lat = b * s
    cb = t_lat // n_ctx
    kt_lat = _chunked_kt(p_all[:t_lat, P_KM:P_KM + M_QK_W], b, nlk)
    kt_ctx = _chunked_kt(p_all[t_lat:, P_KM:P_KM + M_QK_W], b, nck)
    kern = functools.partial(_mlstm_kernel, n_ctx_chunks=nck, n_lat_chunks=nlk)
    return pl.pallas_call(
        kern,
        grid=(b, M_HEADS),
        in_specs=[pl.BlockSpec((s, M_DQK), lambda bi, h: (bi, P_QM // M_DQK + h)),
                  pl.BlockSpec((s, M_DQK), lambda bi, h: (bi, P_KM // M_DQK + h)),
                  pl.BlockSpec((1, 1, nlk, M_DQK, L), lambda bi, h: (bi, h, 0, 0, 0)),
                  pl.BlockSpec((s, M_DV), lambda bi, h: (bi, P_VM // M_DV + h)),
                  pl.BlockSpec((s, M_DV), lambda bi, h: (bi, P_OM // M_DV + h)),
                  pl.BlockSpec((n_ctx, M_DQK), lambda bi, h: (cb + bi, P_KM // M_DQK + h)),
                  pl.BlockSpec((1, 1, nck, M_DQK, L), lambda bi, h: (bi, h, 0, 0, 0)),
                  pl.BlockSpec((n_ctx, M_DV), lambda bi, h: (cb + bi, P_VM // M_DV + h)),
                  pl.BlockSpec((1, 1, nck + nlk, L, 4), lambda bi, h: (bi, h, 0, 0, 0)),
                  pl.BlockSpec((1, 1, nck + nlk, 4, L), lambda bi, h: (bi, h, 0, 0, 0)),
                  pl.BlockSpec((1, M_DV), lambda bi, h: (0, h))],
        out_specs=pl.BlockSpec((s, M_DV), lambda bi, h: (bi, h)),
        out_shape=jax.ShapeDtypeStruct((t_lat, M_V_W), BF16),
        scratch_shapes=[pltpu.VMEM((2, M_DQK, M_DV), F32),
                        pltpu.VMEM((2, 1, M_DQK), F32),
                        pltpu.VMEM((2, 1, LANES), F32),
                        pltpu.VMEM((s, M_DV), F32),
                        pltpu.VMEM((s, M_DV), F32)],
        compiler_params=_cparams(("parallel", "parallel")),
        name="mlstm",
    )(p_all, p_all, kt_lat, p_all, p_all, p_all, kt_ctx, p_all, gcol, grow, norm_g.reshape(1, M_V_W))


def _rope(t, cos_t, sin_s, lane_lo):
    swapped = jnp.where(lane_lo, pltpu.roll(t, A_DH - ROPE_HALF, 1), pltpu.roll(t, ROPE_HALF, 1))
    return t * cos_t + swapped * sin_s


def _attn_kernel(sink_ref, q_ref, k_ref, v_ref, kc_ref, vc_ref, cos_ref, sin_ref, out_ref, kpad, vpad):
    s = k_ref.shape[0]
    blk = A_BLOCK
    hk = pl.program_id(1)
    scale = A_DH ** -0.5
    lane = lax.broadcasted_iota(jnp.int32, (1, A_DH), 1)
    lane_lo = (lane % (2 * ROPE_HALF)) < ROPE_HALF

    zero = jnp.zeros((blk, A_DH), BF16)
    kpad[0:blk, :] = zero
    kpad[s + blk:s + 2 * blk, :] = zero
    vpad[0:blk, :] = zero
    vpad[s + blk:s + 2 * blk, :] = zero
    vpad[blk:s + blk, :] = v_ref[...]

    def rope_k(n, carry):
        r0 = pl.multiple_of(n * blk, blk)
        kf = k_ref[pl.ds(r0, blk), :].astype(F32)
        kr = _rope(kf, cos_ref[pl.ds(r0, blk), :], sin_ref[pl.ds(r0, blk), :], lane_lo)
        kpad[pl.ds(r0 + blk, blk), :] = kr.astype(BF16)
        return carry

    lax.fori_loop(0, s // blk, rope_k, 0)

    kc = kc_ref[...]
    vc = vc_ref[...]
    n_ctx = kc.shape[0]
    n_blocks = s // blk
    rr = lax.broadcasted_iota(jnp.int32, (blk, blk), 0)
    cc = lax.broadcasted_iota(jnp.int32, (blk, blk), 1)
    nt = (((1,), (1,)), ((), ()))
    log2e = 1.4426950408889634
    neg = -jnp.inf

    def body(n, carry):
        r0 = pl.multiple_of(n * blk, blk)
        cos_q = cos_ref[pl.ds(r0, blk), :]
        sin_q = sin_ref[pl.ds(r0, blk), :]
        kb = kpad[pl.ds(r0, 3 * blk), :]
        vb = vpad[pl.ds(r0, 3 * blk), :]
        bias_prev = jnp.where(cc >= rr, jnp.where(n > 0, 0.0, neg), neg)
        bias_next = jnp.where(cc <= rr, jnp.where(n < n_blocks - 1, 0.0, neg), neg)
        q_all = jnp.concatenate(
            [_rope(q_ref[pl.ds(r0, blk), g * A_DH:(g + 1) * A_DH].astype(F32), cos_q, sin_q,
                   lane_lo).astype(BF16) for g in range(A_GROUP)], axis=0)
        s_loc = lax.dot_general(q_all, kb, nt, preferred_element_type=F32)
        s_ctx = lax.dot_general(q_all, kc, nt, preferred_element_type=F32)
        p_parts, inv_den = [], []
        for g in range(A_GROUP):
            rows = slice(g * blk, (g + 1) * blk)
            tiles = ([s_loc[rows, 0:blk] + bias_prev, s_loc[rows, blk:2 * blk],
                      s_loc[rows, 2 * blk:3 * blk] + bias_next]
                     + [s_ctx[rows, c:c + blk] for c in range(0, n_ctx, blk)])
            tmax = functools.reduce(jnp.maximum, tiles)
            snk = sink_ref[hk * A_GROUP + g]
            m = jnp.maximum(jnp.max(tmax, axis=1, keepdims=True) * scale, snk)
            mb = m * log2e
            p = [jnp.exp2(t * (scale * log2e) - mb) for t in tiles]
            den = jnp.sum(functools.reduce(jnp.add, p), axis=1, keepdims=True) + jnp.exp(snk - m)
            inv_den.append(1.0 / den)
            p_parts.append(jnp.concatenate([t.astype(BF16) for t in p], axis=1))
        p_all = jnp.concatenate(p_parts, axis=0)
        o_all = (jnp.dot(p_all[:, :3 * blk], vb, preferred_element_type=F32)
                 + jnp.dot(p_all[:, 3 * blk:], vc, preferred_element_type=F32))
        for g in range(A_GROUP):
            o = o_all[g * blk:(g + 1) * blk, :] * inv_den[g]
            out_ref[pl.ds(r0, blk), g * A_DH:(g + 1) * A_DH] = o.astype(out_ref.dtype)
        return carry

    lax.fori_loop(0, n_blocks, body, 0)


def _attention(p_all, b, s, n_ctx, sink, cos_t, sin_s):
    gw = A_GROUP * A_DH
    cb = (b * s) // n_ctx
    return pl.pallas_call(
        _attn_kernel,
        grid=(b, A_KV_HEADS),
        in_specs=[pl.BlockSpec(memory_space=pltpu.SMEM),
                  pl.BlockSpec((s, gw), lambda bi, h: (bi, P_QA // gw + h)),
                  pl.BlockSpec((s, A_DH), lambda bi, h: (bi, P_KA // A_DH + h)),
                  pl.BlockSpec((s, A_DH), lambda bi, h: (bi, P_VA // A_DH + h)),
                  pl.BlockSpec((n_ctx, A_DH), lambda bi, h: (cb + bi, P_KA // A_DH + h)),
                  pl.BlockSpec((n_ctx, A_DH), lambda bi, h: (cb + bi, P_VA // A_DH + h)),
                  pl.BlockSpec((s, A_DH), lambda bi, h: (0, 0)),
                  pl.BlockSpec((s, A_DH), lambda bi, h: (0, 0))],
        out_specs=pl.BlockSpec((s, gw), lambda bi, h: (bi, h)),
        out_shape=jax.ShapeDtypeStruct((b * s, A_Q_W), BF16),
        scratch_shapes=[pltpu.VMEM((s + 2 * A_BLOCK, A_DH), BF16),
                        pltpu.VMEM((s + 2 * A_BLOCK, A_DH), BF16)],
        compiler_params=_cparams(("parallel", "parallel")),
        name="attention",
    )(sink, p_all, p_all, p_all, p_all, p_all, cos_t, sin_s)


def _rope_tables(n_tokens):
    rows = n_tokens // GRID_W
    row = jnp.repeat(jnp.arange(rows, dtype=F32), GRID_W)
    col = jnp.tile(jnp.arange(GRID_W, dtype=F32), rows)
    inv_freq = ROPE_BASE ** (-jnp.arange(ROPE_HALF, dtype=F32) / ROPE_HALF)
    ar = row[:, None] * inv_freq
    ac = col[:, None] * inv_freq
    cos_t = jnp.concatenate([jnp.cos(ar), jnp.cos(ar), jnp.cos(ac), jnp.cos(ac)], axis=1)
    sin_s = jnp.concatenate([-jnp.sin(ar), jnp.sin(ar), -jnp.sin(ac), jnp.sin(ac)], axis=1)
    return cos_t, sin_s


def _merge_kernel(hm_ref, at_ref, wm_ref, wa_ref, gm_ref, ga_ref, o_ref):
    bm = jnp.dot(hm_ref[...], wm_ref[...], preferred_element_type=F32)
    ba = jnp.dot(at_ref[...], wa_ref[...], preferred_element_type=F32)
    o = jax.nn.sigmoid(gm_ref[...].astype(F32)) * bm + jax.nn.sigmoid(ga_ref[...].astype(F32)) * ba
    o_ref[...] = o.astype(o_ref.dtype)


def _merge(hm, att, p2d, w_br_m, w_br_a):
    t, d = hm.shape
    tm, tn = 1024, 1024
    assert P_GBM % tn == 0 and P_GBA % tn == 0
    return pl.pallas_call(
        _merge_kernel,
        grid=(t // tm, d // tn),
        in_specs=[pl.BlockSpec((tm, M_V_W), lambda i, j: (i, 0)),
                  pl.BlockSpec((tm, A_Q_W), lambda i, j: (i, 0)),
                  pl.BlockSpec((M_V_W, tn), lambda i, j: (0, j)),
                  pl.BlockSpec((A_Q_W, tn), lambda i, j: (0, j)),
                  pl.BlockSpec((tm, tn), lambda i, j: (i, P_GBM // tn + j)),
                  pl.BlockSpec((tm, tn), lambda i, j: (i, P_GBA // tn + j))],
        out_specs=pl.BlockSpec((tm, tn), lambda i, j: (i, j)),
        out_shape=jax.ShapeDtypeStruct((t, d), BF16),
        compiler_params=_cparams(("parallel", "arbitrary")),
        name="merge",
    )(hm, att, w_br_m, w_br_a, p2d, p2d)


def _outproj_kernel(mg_ref, x_ref, wo_ref, g1_ref, ng_ref, sh_ref, sc_ref, wr_ref, br_ref,
                    x1_ref, h2_ref, ri_ref, rw_ref):
    y = jnp.dot(mg_ref[0], wo_ref[...], preferred_element_type=F32)
    x1 = x_ref[0] + g1_ref[0] * y
    x1_ref[0] = x1
    hn = x1 * lax.rsqrt(jnp.mean(x1 * x1, axis=-1, keepdims=True) + EPS) * ng_ref[...]
    h2 = hn * (1.0 + sc_ref[0]) + sh_ref[0]
    h2_ref[0] = h2

    logits = jnp.dot(h2.astype(BF16), wr_ref[...], preferred_element_type=F32) + br_ref[...]
    lane = lax.broadcasted_iota(jnp.int32, logits.shape, 1)
    neg = -jnp.inf
    gl = jnp.where(lane < N_GROUPS, logits, neg)
    gmax = jnp.max(gl, axis=1, keepdims=True)
    gsel = jnp.min(jnp.where(gl == gmax, lane, LANES), axis=1, keepdims=True)
    p_sel = 1.0 / jnp.sum(jnp.exp(gl - gmax), axis=1, keepdims=True)
    lo = N_GROUPS + gsel * EXPERTS_PER_GROUP
    el = jnp.where((lane >= lo) & (lane < lo + EXPERTS_PER_GROUP), logits, neg)
    v1 = jnp.max(el, axis=1, keepdims=True)
    i1 = jnp.min(jnp.where(el == v1, lane, LANES), axis=1, keepdims=True)
    el2 = jnp.where(lane == i1, neg, el)
    v2 = jnp.max(el2, axis=1, keepdims=True)
    i2 = jnp.min(jnp.where(el2 == v2, lane, LANES), axis=1, keepdims=True)
    e2 = jnp.exp(v2 - v1)
    w1 = p_sel / (1.0 + e2)
    w2 = p_sel * e2 / (1.0 + e2)
    ri_ref[0] = jnp.where(lane == 0, i1 - N_GROUPS, jnp.where(lane == 1, i2 - N_GROUPS, 0))
    rw_ref[0] = jnp.where(lane == 0, w1, jnp.where(lane == 1, w2, 0.0))


def _outproj(merged, x, w_out, g1, norm_g, shift, scale, w_r, b_r):
    b, s, d = x.shape
    tm = 512
    row = lambda bi, i: (bi, i, 0)
    per_b = lambda bi, i: (bi, 0, 0)
    fixed = lambda bi, i: (0, 0)
    return pl.pallas_call(
        _outproj_kernel,
        grid=(b, s // tm),
        in_specs=[pl.BlockSpec((1, tm, d), row),
                  pl.BlockSpec((1, tm, d), row),
                  pl.BlockSpec((d, d), fixed),
                  pl.BlockSpec((1, 1, d), per_b),
                  pl.BlockSpec((1, d), fixed),
                  pl.BlockSpec((1, 1, d), per_b),
                  pl.BlockSpec((1, 1, d), per_b),
                  pl.BlockSpec((d, LANES), fixed),
                  pl.BlockSpec((1, LANES), fixed)],
        out_specs=[pl.BlockSpec((1, tm, d), row),
                   pl.BlockSpec((1, tm, d), row),
                   pl.BlockSpec((1, tm, LANES), row),
                   pl.BlockSpec((1, tm, LANES), row)],
        out_shape=[jax.ShapeDtypeStruct((b, s, d), F32),
                   jax.ShapeDtypeStruct((b, s, d), F32),
                   jax.ShapeDtypeStruct((b, s, LANES), jnp.int32),
                   jax.ShapeDtypeStruct((b, s, LANES), F32)],
        compiler_params=_cparams(("parallel", "parallel")),
        name="outproj_router",
    )(merged, x, w_out, g1, norm_g.reshape(1, d), shift, scale, w_r, b_r)


def _moe_kernel(sbe_ref, nsub_ref, base_ref, nused_ref, tok_ref, h_hbm, wg_hbm, wu_hbm, wd_hbm, y_hbm,
                xbuf, xb, ring_gu, ring_d, wb_gu, wb_d, hgu, act, ostage, sem, wsem, osem, *, n_super):
    nused = nused_ref[0]
    d = h_hbm.shape[1]
    kh = d // 2
    fh = D_EXPERT // 2

    def w_copy(c, sb):
        e = sbe_ref[sb]
        if c < 4:
            w_hbm = wg_hbm if c % 2 == 0 else wu_hbm
            return pltpu.make_async_copy(w_hbm.at[0, e, pl.ds((c // 2) * kh, kh), :], ring_gu.at[c],
                                         wsem.at[c])
        return pltpu.make_async_copy(wd_hbm.at[0, e, pl.ds((c - 4) * fh, fh), :], ring_d.at[c - 4],
                                     wsem.at[c])

    def out_copy(sb, p):
        return pltpu.make_async_copy(ostage.at[p], y_hbm.at[pl.ds(sb * MOE_SUPER, MOE_SUPER)], osem.at[p])

    def start_gather(sb, slot):
        base = base_ref[sb]

        def issue(i, carry):
            for u in range(GATHER_UNROLL):
                r = i * GATHER_UNROLL + u
                tok = tok_ref[base + r]
                pltpu.make_async_copy(h_hbm.at[pl.ds(tok, 1)], xbuf.at[slot, pl.ds(r, 1)],
                                      sem.at[slot]).start()
            return carry

        lax.fori_loop(0, nsub_ref[sb] * (MOE_SUB // GATHER_UNROLL), issue, 0)

    def wait_gather(sb, slot):
        def wait_sub(i, carry):
            pltpu.make_async_copy(h_hbm.at[pl.ds(0, MOE_SUB)], xbuf.at[slot, pl.ds(0, MOE_SUB)],
                                  sem.at[slot]).wait()
            return carry

        lax.fori_loop(0, nsub_ref[sb], wait_sub, 0)

    def job_compute(c, n, p):
        def run(m):
            rows = slice(0, m * MOE_SUB)
            if c < 4:
                part = jnp.dot(xb[rows, (c // 2) * kh:(c // 2 + 1) * kh], wb_gu[...],
                               preferred_element_type=F32)
                if c < 2:
                    hgu[c, rows, :] = part
                elif c == 2:
                    hgu[0, rows, :] += part
                else:
                    g = hgu[0, rows, :]
                    act[rows, :] = (g * jax.nn.sigmoid(g) * (hgu[1, rows, :] + part)).astype(BF16)
            else:
                part = jnp.dot(act[rows, (c - 4) * fh:(c - 3) * fh], wb_d[...],
                               preferred_element_type=F32)
                if c == 4:
                    hgu[0, rows, :] = part[:, :kh]
                    hgu[1, rows, :] = part[:, kh:]
                else:
                    y = part + jnp.concatenate([hgu[0, rows, :], hgu[1, rows, :]], axis=1)
                    ostage[p, rows, :] = y.astype(ostage.dtype)

        lax.switch(n - 1, [functools.partial(run, m) for m in range(1, MOE_SUPER // MOE_SUB + 1)])

    ostage[...] = jnp.zeros_like(ostage)
    start_gather(0, 0)
    for c in range(6):
        w_copy(c, 0).start()

    def superblock(s, carry):
        slot = s % 2
        n = nsub_ref[s]
        wait_gather(s, slot)

        @pl.when(s + 1 < nused)
        def _():
            start_gather(s + 1, 1 - slot)

        def cast_x(i, c2):
            rows = pl.ds(pl.multiple_of(i * MOE_SUB, MOE_SUB), MOE_SUB)
            xb[rows, :] = xbuf[slot, rows, :].astype(BF16)
            return c2

        lax.fori_loop(0, n, cast_x, 0)

        for c in range(6):
            w_copy(c, s).wait()
            if c < 4:
                wb_gu[...] = ring_gu[c].astype(BF16)
            else:
                wb_d[...] = ring_d[c - 4].astype(BF16)

            @pl.when(s + 1 < nused)
            def _():
                w_copy(c, s + 1).start()

            if c == 5:
                @pl.when(s >= 2)
                def _():
                    out_copy(s - 2, slot).wait()

            job_compute(c, n, slot)

        out_copy(s, slot).start()
        return carry

    lax.fori_loop(0, nused, superblock, 0)

    @pl.when(nused >= 2)
    def _():
        out_copy(nused - 2, nused % 2).wait()

    out_copy(nused - 1, (nused - 1) % 2).wait()

    ostage[0] = jnp.zeros(ostage.shape[1:], ostage.dtype)

    def zero_start(sb, carry):
        out_copy(sb, 0).start()
        return carry

    def zero_wait(sb, carry):
        out_copy(sb, 0).wait()
        return carry

    lax.fori_loop(nused, n_super, zero_start, 0)
    lax.fori_loop(nused, n_super, zero_wait, 0)


def _moe_experts(h2, sorted_tok, sb_expert, sb_nsub, sb_base, n_used, n_super, w_g, w_u, w_d):
    _, d = h2.shape
    assert d % 2 == 0 and D_EXPERT % 2 == 0 and d // 2 == D_EXPERT
    any_spec = pl.BlockSpec(memory_space=pl.ANY)
    grid_spec = pltpu.PrefetchScalarGridSpec(
        num_scalar_prefetch=5,
        grid=(1,),
        in_specs=[any_spec, any_spec, any_spec, any_spec],
        out_specs=any_spec,
        scratch_shapes=[pltpu.VMEM((2, MOE_SUPER, d), F32),
                        pltpu.VMEM((MOE_SUPER, d), BF16),
                        pltpu.VMEM((4, d // 2, D_EXPERT), F32),
                        pltpu.VMEM((2, D_EXPERT // 2, d), F32),
                        pltpu.VMEM((d // 2, D_EXPERT), BF16),
                        pltpu.VMEM((D_EXPERT // 2, d), BF16),
                        pltpu.VMEM((2, MOE_SUPER, D_EXPERT), F32),
                        pltpu.VMEM((MOE_SUPER, D_EXPERT), BF16),
                        pltpu.VMEM((2, MOE_SUPER, d), BF16),
                        pltpu.SemaphoreType.DMA((2,)),
                        pltpu.SemaphoreType.DMA((6,)),
                        pltpu.SemaphoreType.DMA((2,))],
    )
    return pl.pallas_call(
        functools.partial(_moe_kernel, n_super=n_super),
        grid_spec=grid_spec,
        out_shape=jax.ShapeDtypeStruct((n_super * MOE_SUPER, d), BF16),
        compiler_params=_cparams(("arbitrary",), vmem=MOE_VMEM_LIMIT),
        name="moe_experts",
    )(sb_expert, sb_nsub, sb_base, n_used, sorted_tok, h2, w_g, w_u, w_d)


def _dispatch_tables(eid, n_super):
    flat = eid.reshape(-1)
    n_assign = flat.shape[0]
    order = jnp.argsort(flat).astype(jnp.int32)
    sorted_tok = jnp.concatenate([order // TOP_K, jnp.zeros((MOE_SUB,), jnp.int32)])
    counts = jnp.bincount(flat, length=N_EXPERTS)
    nsup = (counts + MOE_SUPER - 1) // MOE_SUPER
    sup_end = jnp.cumsum(nsup)
    sup_start = sup_end - nsup
    start = jnp.cumsum(counts) - counts
    off = sup_start * MOE_SUPER - start
    d_off = jnp.diff(off, prepend=0)
    pos = jnp.arange(n_assign)
    off_sorted = jnp.sum(jnp.where(pos[:, None] >= start[None, :], d_off[None, :], 0), axis=1)
    dest_sorted = (pos + off_sorted).astype(jnp.int32)
    _, dest = lax.sort((order, dest_sorted), num_keys=1)
    sidx = jnp.arange(n_super)
    sb_expert = jnp.minimum(jnp.sum(sup_end[None, :] <= sidx[:, None], axis=1), N_EXPERTS - 1)
    n_used = sup_end[-1]
    k = sidx - sup_start[sb_expert]
    rows_in = jnp.clip(counts[sb_expert] - k * MOE_SUPER, 0, MOE_SUPER)
    sb_nsub = jnp.where(sidx < n_used, (rows_in + MOE_SUB - 1) // MOE_SUB, 0)
    sb_base = start[sb_expert] + k * MOE_SUPER
    i32 = lambda a: a.astype(jnp.int32)
    return (sorted_tok, dest.reshape(-1, TOP_K), i32(sb_expert), i32(sb_nsub), i32(sb_base),
            i32(n_used).reshape(1))


def _final_kernel(x1_ref, ya_ref, yb_ref, w_ref, g2_ref, fg_ref, o_ref):
    w = w_ref[0]
    moe = w[:, 0:1] * ya_ref[0].astype(F32) + w[:, 1:2] * yb_ref[0].astype(F32)
    x2 = x1_ref[0] + g2_ref[0] * moe
    o_ref[0] = x2 * lax.rsqrt(jnp.mean(x2 * x2, axis=-1, keepdims=True) + EPS) * fg_ref[...]


def _final(x1, ya, yb, w_top, g2, final_g):
    b, s, d = x1.shape
    ts = 512
    row = lambda bi, i: (bi, i, 0)
    return pl.pallas_call(
        _final_kernel,
        grid=(b, s // ts),
        in_specs=[pl.BlockSpec((1, ts, d), row),
                  pl.BlockSpec((1, ts, d), row),
                  pl.BlockSpec((1, ts, d), row),
                  pl.BlockSpec((1, ts, TOP_K), row),
                  pl.BlockSpec((1, 1, d), lambda bi, i: (bi, 0, 0)),
                  pl.BlockSpec((1, d), lambda bi, i: (0, 0))],
        out_specs=pl.BlockSpec((1, ts, d), row),
        out_shape=jax.ShapeDtypeStruct((b, s, d), F32),
        compiler_params=_cparams(("parallel", "parallel")),
        name="final_norm",
    )(x1, ya, yb, w_top, g2, final_g.reshape(1, d))


def kernel(x, c, ctx, c_ctx, w_mod, b_mod, norm1_g, w_in, mlstm_gate_b, mlstm_norm_g, attn_sink, w_br_m, w_br_a, w_out, norm2_g, w_router_grp, b_router_grp, w_router_exp, b_router_exp, w_exp_gate, w_exp_up, w_exp_down, final_norm_g):
    b, s, d = x.shape
    n_ctx = ctx.shape[1]
    assert w_mod.shape[0] == 1, "single-layer block"

    pad_rows = 8 - (b + 1)
    cvec = jnp.concatenate([c, c_ctx[None, :], jnp.zeros((pad_rows, d), F32)], axis=0)
    mods = _adaln(cvec, w_mod[0], b_mod[0])
    g1, sh2, sc2, g2 = [mods[:b, i * d:(i + 1) * d].reshape(b, 1, d) for i in range(2, 6)]
    sh1_all = mods[:, 0:d].reshape(-1, 1, d)
    sc1_all = mods[:, d:2 * d].reshape(-1, 1, d)

    t_lat = b * s
    h_all = _norm_mod(x.reshape(t_lat, d), ctx.reshape(b * n_ctx, d), norm1_g[0], sh1_all, sc1_all, s, b)
    w_in_t = jnp.transpose(w_in[0])
    p_all = _proj(h_all, w_in_t, t_lat)
    b_gate = jnp.pad(mlstm_gate_b[0].astype(F32).reshape(1, GATE_W), ((0, 0), (0, LANES - GATE_W)))
    g_all = _gate_proj(h_all, w_in_t, b_gate)

    g_lat = g_all[:t_lat, :GATE_W].reshape(b, s // M_CHUNK, M_CHUNK, 4, M_HEADS)
    g_ctx = g_all[t_lat:, :GATE_W].reshape(b, n_ctx // M_CHUNK, M_CHUNK, 4, M_HEADS)
    gates = jnp.concatenate([g_ctx, g_lat], axis=1)
    gcol = jnp.transpose(gates, (0, 4, 1, 2, 3))
    grow = jnp.transpose(gates, (0, 4, 1, 3, 2))

    hm = _mlstm(p_all, b, s, n_ctx, gcol, grow, mlstm_norm_g[0])
    cos_t, sin_s = _rope_tables(s)
    att = _attention(p_all, b, s, n_ctx, attn_sink[0], cos_t, sin_s)

    merged = _merge(hm, att, p_all, w_br_m[0].astype(BF16), w_br_a[0].astype(BF16))

    w_r = jnp.concatenate([w_router_grp[0], w_router_exp[0]], axis=1)
    n_r = w_r.shape[1]
    w_r = jnp.pad(w_r, ((0, 0), (0, LANES - n_r))).astype(BF16)
    b_r = jnp.pad(jnp.concatenate([b_router_grp[0], b_router_exp[0]]), (0, LANES - n_r)).reshape(1, LANES)
    x1, h2, r_idx, r_w = _outproj(merged.reshape(b, s, d), x, w_out[0].astype(BF16), g1, norm2_g[0],
                                  sh2, sc2, w_r, b_r)

    t_tok = b * s
    eid = r_idx.reshape(t_tok, LANES)[:, :TOP_K]
    w_top = r_w[:, :, :TOP_K]
    n_super = (t_tok * TOP_K) // MOE_SUPER + N_EXPERTS
    sorted_tok, dest, sb_expert, sb_nsub, sb_base, n_used = _dispatch_tables(eid, n_super)
    ybuf = _moe_experts(h2.reshape(t_tok, d), sorted_tok, sb_expert, sb_nsub, sb_base, n_used, n_super,
                        w_exp_gate, w_exp_up, w_exp_down)
    ya = ybuf.at[dest[:, 0]].get(mode="promise_in_bounds").reshape(b, s, d)
    yb = ybuf.at[dest[:, 1]].get(mode="promise_in_bounds").reshape(b, s, d)

    return _final(x1, ya, yb, w_top, g2, final_norm_g)
```

```python
import functools
import math

import jax
import jax.numpy as jnp
from jax import lax
from jax.experimental import pallas as pl
from jax.experimental.pallas import tpu as pltpu

F32 = jnp.float32
BF16 = jnp.bfloat16

D_MODEL = 2048
EPS = 1e-6
GRID_W = 64

M_HEADS = 8
M_DQK = 128
M_DV = 256
M_CHUNK = 256
MLSTM_ROWS = 256
MLSTM_UNROLL = 1
assert M_DQK == 128, "mLSTM state rows are laid out one key dim per lane"

A_HEADS = 16
A_KV_HEADS = 4
A_GROUP = A_HEADS // A_KV_HEADS
A_DH = 128
A_BLOCK = 128
WINDOW = 128
ROPE_HALF = A_DH // 4
ROPE_BASE = 10000.0
assert WINDOW == A_BLOCK, "band masks assume the window equals the block size"
ATTN_UNROLL = 4
OUTPROJ_SPLIT = 2

N_GROUPS = 8
EXPERTS_PER_GROUP = 8
N_EXPERTS = N_GROUPS * EXPERTS_PER_GROUP
TOP_K = 2
D_EXPERT = 1024

M_QK_W = M_HEADS * M_DQK
M_V_W = M_HEADS * M_DV
A_Q_W = A_HEADS * A_DH
A_KV_W = A_KV_HEADS * A_DH
GATE_W = 4 * M_HEADS

OFF_QM = 0
OFF_KM = OFF_QM + M_QK_W
OFF_VM = OFF_KM + M_QK_W
OFF_OM = OFF_VM + M_V_W
OFF_GM = OFF_OM + M_V_W
OFF_QA = OFF_GM + GATE_W
OFF_KA = OFF_QA + A_Q_W
OFF_VA = OFF_KA + A_KV_W
OFF_GBR = OFF_VA + A_KV_W
IN_WIDTH = OFF_GBR + 2 * D_MODEL

P_QM = 0
P_KM = P_QM + M_QK_W
P_VM = P_KM + M_QK_W
P_OM = P_VM + M_V_W
P_QA = P_OM + M_V_W
P_KA = P_QA + A_Q_W
P_VA = P_KA + A_KV_W
P_GBM = P_VA + A_KV_W
P_GBA = P_GBM + D_MODEL
P_WIDTH = P_GBA + D_MODEL

LANES = 128
NORM_ROWS = 512
PROJ_TM = 1024
PROJ_TN = 1024
MOE_SUB = 128
MOE_SUPER = 512
GATHER_UNROLL = 8
VMEM_LIMIT = 56 * 1024 * 1024
MOE_VMEM_LIMIT = 60 * 1024 * 1024


def _cparams(sem, vmem=VMEM_LIMIT):
    return pltpu.CompilerParams(dimension_semantics=sem, vmem_limit_bytes=vmem)


def _adaln_kernel(c_ref, w_ref, b_ref, o_ref):
    c = c_ref[...]
    a = (c * jax.nn.sigmoid(c)).astype(BF16)
    o_ref[...] = jnp.dot(a, w_ref[...].astype(BF16), preferred_element_type=F32) + b_ref[...]


def _adaln(cvec, w_mod, b_mod):
    rows, d = cvec.shape
    n = w_mod.shape[1]
    tn = 1024
    return pl.pallas_call(
        _adaln_kernel,
        grid=(n // tn,),
        in_specs=[pl.BlockSpec((rows, d), lambda j: (0, 0)),
                  pl.BlockSpec((d, tn), lambda j: (0, j)),
                  pl.BlockSpec((1, tn), lambda j: (0, j))],
        out_specs=pl.BlockSpec((rows, tn), lambda j: (0, j)),
        out_shape=jax.ShapeDtypeStruct((rows, n), F32),
        compiler_params=_cparams(("parallel",)),
        name="adaln",
    )(cvec, w_mod, b_mod.reshape(1, n))


def _norm_mod_kernel(x_ref, c_ref, g_ref, sh_ref, sc_ref, o_ref, *, n_lat_tiles):
    xv = jnp.where(pl.program_id(0) < n_lat_tiles, x_ref[...], c_ref[...])
    y = xv * lax.rsqrt(jnp.mean(xv * xv, axis=-1, keepdims=True) + EPS) * g_ref[...]
    o_ref[...] = (y * (1.0 + sc_ref[0]) + sh_ref[0]).astype(o_ref.dtype)


def _norm_mod(x2d, ctx2d, g, shift, scale, rows_per_batch, ctx_mod_row):
    t_lat, d = x2d.shape
    t_ctx = ctx2d.shape[0]
    tm = NORM_ROWS
    n_lat, n_ctx = t_lat // tm, t_ctx // tm
    tiles_per_batch = rows_per_batch // tm
    mod_map = lambda i: (jnp.where(i < n_lat, i // tiles_per_batch, ctx_mod_row), 0, 0)
    return pl.pallas_call(
        functools.partial(_norm_mod_kernel, n_lat_tiles=n_lat),
        grid=(n_lat + n_ctx,),
        in_specs=[pl.BlockSpec((tm, d), lambda i: (jnp.minimum(i, n_lat - 1), 0)),
                  pl.BlockSpec((tm, d), lambda i: (jnp.maximum(i - n_lat, 0), 0)),
                  pl.BlockSpec((1, d), lambda i: (0, 0)),
                  pl.BlockSpec((1, 1, d), mod_map),
                  pl.BlockSpec((1, 1, d), mod_map)],
        out_specs=pl.BlockSpec((tm, d), lambda i: (i, 0)),
        out_shape=jax.ShapeDtypeStruct((t_lat + t_ctx, d), BF16),
        compiler_params=_cparams(("parallel",)),
        name="norm_mod",
    )(x2d, ctx2d, g.reshape(1, d), shift, scale)


def _proj_kernel(h_ref, wa_ref, wb_ref, o_ref, w_bf, *, first_shifted, n_ctx_tiles, ctx_col_tiles):
    j = pl.program_id(0)
    tn = w_bf.shape[0]

    @pl.when(pl.program_id(1) == 0)
    def _():
        @pl.when(j < first_shifted)
        def _():
            w_bf[...] = wa_ref[...].astype(BF16)

        @pl.when(j >= first_shifted)
        def _():
            w_bf[0:tn - GATE_W, :] = wa_ref[GATE_W:tn, :].astype(BF16)
            w_bf[tn - GATE_W:tn, :] = wb_ref[...].astype(BF16)

    needed = pl.program_id(1) >= n_ctx_tiles
    for lo, hi in ctx_col_tiles:
        needed = needed | ((j >= lo) & (j < hi))

    @pl.when(needed)
    def _():
        o_ref[...] = lax.dot_general(h_ref[...], w_bf[...], (((1,), (1,)), ((), ())),
                                     preferred_element_type=F32).astype(o_ref.dtype)

    @pl.when(jnp.logical_not(needed))
    def _():
        o_ref[...] = jnp.zeros_like(o_ref)


def _proj(h_all, w_in_t, t_lat):
    t, d = h_all.shape
    tm, tn = PROJ_TM, PROJ_TN
    assert OFF_GM % tn == 0 and P_WIDTH % tn == 0 and t % tm == 0 and tn % GATE_W == 0 and t_lat % tm == 0
    ctx_col_tiles = ((P_KM // tn, -(-P_OM // tn)), (P_KA // tn, -(-P_GBM // tn)))
    n_tiles, n_lat = t // tm, t_lat // tm
    row_tile = lambda i: (i + n_lat) % n_tiles
    return pl.pallas_call(
        functools.partial(_proj_kernel, first_shifted=OFF_GM // tn, n_ctx_tiles=n_tiles - n_lat,
                          ctx_col_tiles=ctx_col_tiles),
        grid=(P_WIDTH // tn, n_tiles),
        in_specs=[pl.BlockSpec((tm, d), lambda j, i: (row_tile(i), 0)),
                  pl.BlockSpec((tn, d), lambda j, i: (j, 0)),
                  pl.BlockSpec((GATE_W, d), lambda j, i: ((j + 1) * (tn // GATE_W), 0))],
        out_specs=pl.BlockSpec((tm, tn), lambda j, i: (row_tile(i), j)),
        out_shape=jax.ShapeDtypeStruct((t, P_WIDTH), BF16),
        scratch_shapes=[pltpu.VMEM((tn, d), BF16)],
        compiler_params=_cparams(("parallel", "arbitrary")),
        name="in_proj",
    )(h_all, w_in_t, w_in_t)


def _log_sigmoid(x):
    return jnp.minimum(x, 0.0) - jnp.log(1.0 + jnp.exp(-jnp.abs(x)))


def _gate_proj_kernel(h_ref, w_ref, b_ref, o_ref):
    wr = lax.broadcasted_iota(jnp.int32, w_ref.shape, 0)
    w = jnp.where(wr < GATE_W, w_ref[...], 0.0).astype(BF16)
    g = lax.dot_general(h_ref[...], w, (((1,), (1,)), ((), ())), preferred_element_type=F32) + b_ref[...]
    lf = _log_sigmoid(g)
    L = g.shape[0]
    row = lax.broadcasted_iota(jnp.int32, (L, L), 0)
    col = lax.broadcasted_iota(jnp.int32, (L, L), 1)
    hi = lax.Precision.HIGHEST
    cum_f = jnp.dot((col <= row).astype(F32), lf, precision=hi, preferred_element_type=F32)
    cum_b = jnp.dot((col >= row).astype(F32), lf, precision=hi, preferred_element_type=F32)
    lane = lax.broadcasted_iota(jnp.int32, g.shape, 1)
    is_ff = (lane >= M_HEADS) & (lane < 2 * M_HEADS)
    is_fb = (lane >= 3 * M_HEADS) & (lane < 4 * M_HEADS)
    o_ref[...] = jnp.where(is_ff, cum_f, jnp.where(is_fb, cum_b, g))


def _gate_proj(h_all, w_in_t, b_gate):
    t, d = h_all.shape
    L = M_CHUNK
    assert OFF_GM % LANES == 0
    return pl.pallas_call(
        _gate_proj_kernel,
        grid=(t // L,),
        in_specs=[pl.BlockSpec((L, d), lambda i: (i, 0)),
                  pl.BlockSpec((LANES, d), lambda i: (OFF_GM // LANES, 0)),
                  pl.BlockSpec((1, LANES), lambda i: (0, 0))],
        out_specs=pl.BlockSpec((L, LANES), lambda i: (i, 0)),
        out_shape=jax.ShapeDtypeStruct((t, LANES), F32),
        compiler_params=_cparams(("parallel",)),
        name="gate_proj",
    )(h_all, w_in_t, b_gate)


def _mlstm_chunk(direction, qv, kv, ktv, vv, gcol, grow, c_ref, n_ref, m_ref, valid):
    gi = 2 * direction
    igr = grow[gi:gi + 1, :]
    b_r = grow[gi + 1:gi + 2, :]
    L = b_r.shape[1]
    wide = lambda a, n: jnp.concatenate([a] * (n // LANES), axis=1)
    b_end = b_r[:, L - 1:L] if direction == 0 else b_r[:, 0:1]
    kscale = M_DQK ** -0.5

    c_st = c_ref[direction]
    n_st = n_ref[direction]
    m_st = m_ref[direction]

    h = None
    if qv is not None:
        log2e = 1.4426950408889634
        log2k = -0.5 * math.log2(M_DQK)
        row_term = (igr - b_r) * log2e + log2k
        c_bf = c_st.astype(BF16)
        parts = []
        for r in range(0, L, MLSTM_ROWS):
            rs = slice(r, r + MLSTM_ROWS)
            q_r = qv[rs]
            b_c = jnp.broadcast_to(gcol[rs, gi + 1:gi + 2], (MLSTM_ROWS, LANES))
            dm2 = jnp.where(valid[rs], wide(b_c * log2e, L) + row_term, -jnp.inf)
            inter = b_c + m_st
            m_t = jnp.maximum(inter, (jnp.max(dm2, axis=1, keepdims=True) - log2k) * (1.0 / log2e))
            w_inter = jnp.exp(inter - m_t)
            sc = lax.dot_general(q_r, kv, (((1,), (1,)), ((), ())), preferred_element_type=F32)
            s = sc * jnp.exp2(dm2 - wide(m_t * log2e, L))
            num = (wide(w_inter, M_DV) * jnp.dot(q_r, c_bf, preferred_element_type=F32)
                   + jnp.dot(s.astype(BF16), vv, preferred_element_type=F32))
            qn = (w_inter * jnp.sum(q_r.astype(F32) * n_st, axis=1, keepdims=True)
                  + jnp.sum(s, axis=1, keepdims=True))
            parts.append(num * wide(1.0 / jnp.maximum(jnp.abs(qn), jnp.exp(-m_t)), M_DV))
        h = jnp.concatenate(parts, axis=0)

    dec = b_end - b_r + igr
    m_new = jnp.maximum(b_end + m_st, jnp.max(dec, axis=1, keepdims=True))
    w_s = jnp.exp(dec - m_new[:, 0:1]) * kscale
    w_c = jnp.exp(b_end + m_st - m_new)
    wk = (ktv.astype(F32) * w_s).astype(BF16)
    c_ref[direction] = wide(w_c, M_DV) * c_st + jnp.dot(wk, vv, preferred_element_type=F32)
    n_ref[direction] = w_c * n_st + jnp.dot(w_s.astype(BF16), kv, preferred_element_type=F32)
    m_ref[direction] = m_new
    return h


def _mlstm_kernel(q_ref, k_ref, kt_ref, v_ref, o_ref, kc_ref, ktc_ref, vc_ref, gcol_ref, grow_ref, ng_ref,
                  out_ref, c_ref, n_ref, m_ref, hf_ref, hb_ref, *, n_ctx_chunks, n_lat_chunks):
    L = M_CHUNK
    row = lax.broadcasted_iota(jnp.int32, (L, L), 0)
    col = lax.broadcasted_iota(jnp.int32, (L, L), 1)
    valid = (col <= row, col >= row)

    c_ref[...] = jnp.zeros_like(c_ref)
    n_ref[...] = jnp.zeros_like(n_ref)
    m_ref[...] = jnp.zeros_like(m_ref)

    for j in range(n_ctx_chunks):
        for direction, jj in ((0, j), (1, n_ctx_chunks - 1 - j)):
            _mlstm_chunk(direction, None, kc_ref[jj * L:(jj + 1) * L, :], ktc_ref[0, 0, jj],
                         vc_ref[jj * L:(jj + 1) * L, :], gcol_ref[0, 0, jj], grow_ref[0, 0, jj],
                         c_ref, n_ref, m_ref, valid[direction])

    def body(i, carry):
        for u in range(MLSTM_UNROLL):
            j = i * MLSTM_UNROLL + u
            for direction, jj, h_ref in ((0, j, hf_ref), (1, n_lat_chunks - 1 - j, hb_ref)):
                r0 = pl.multiple_of(jj * L, L)
                h = _mlstm_chunk(direction, q_ref[pl.ds(r0, L), :], k_ref[pl.ds(r0, L), :],
                                 kt_ref[0, 0, jj], v_ref[pl.ds(r0, L), :],
                                 gcol_ref[0, 0, n_ctx_chunks + jj], grow_ref[0, 0, n_ctx_chunks + jj],
                                 c_ref, n_ref, m_ref, valid[direction])
                h_ref[pl.ds(r0, L), :] = h
        return carry

    lax.fori_loop(0, n_lat_chunks // MLSTM_UNROLL, body, 0)

    def finish(j, carry):
        r0 = pl.multiple_of(j * L, L)
        hh = hf_ref[pl.ds(r0, L), :] + hb_ref[pl.ds(r0, L), :]
        hh = hh * lax.rsqrt(jnp.mean(hh * hh, axis=-1, keepdims=True) + EPS)
        og = jax.nn.sigmoid(o_ref[pl.ds(r0, L), :].astype(F32))
        out_ref[pl.ds(r0, L), :] = (hh * ng_ref[...] * og).astype(out_ref.dtype)
        return carry

    lax.fori_loop(0, n_lat_chunks, finish, 0)


def _chunked_kt(k2d, b, n_chunks):
    k = k2d.reshape(b, n_chunks, M_CHUNK, M_HEADS, M_DQK)
    return jnp.transpose(k, (0, 3, 1, 4, 2))


def _mlstm(p_all, b, s, n_ctx, gcol, grow, norm_g):
    L = M_CHUNK
    nck, nlk = n_ctx // L, s // L
    t_lat = b * s
    cb = t_lat // n_ctx
    kt_lat = _chunked_kt(p_all[:t_lat, P_KM:P_KM + M_QK_W], b, nlk)
    kt_ctx = _chunked_kt(p_all[t_lat:, P_KM:P_KM + M_QK_W], b, nck)
    kern = functools.partial(_mlstm_kernel, n_ctx_chunks=nck, n_lat_chunks=nlk)
    return pl.pallas_call(
        kern,
        grid=(b, M_HEADS),
        in_specs=[pl.BlockSpec((s, M_DQK), lambda bi, h: (bi, P_QM // M_DQK + h)),
                  pl.BlockSpec((s, M_DQK), lambda bi, h: (bi, P_KM // M_DQK + h)),
                  pl.BlockSpec((1, 1, nlk, M_DQK, L), lambda bi, h: (bi, h, 0, 0, 0)),
                  pl.BlockSpec((s, M_DV), lambda bi, h: (bi, P_VM // M_DV + h)),
                  pl.BlockSpec((s, M_DV), lambda bi, h: (bi, P_OM // M_DV + h)),
                  pl.BlockSpec((n_ctx, M_DQK), lambda bi, h: (cb + bi, P_KM // M_DQK + h)),
                  pl.BlockSpec((1, 1, nck, M_DQK, L), lambda bi, h: (bi, h, 0, 0, 0)),
                  pl.BlockSpec((n_ctx, M_DV), lambda bi, h: (cb + bi, P_VM // M_DV + h)),
                  pl.BlockSpec((1, 1, nck + nlk, L, 4), lambda bi, h: (bi, h, 0, 0, 0)),
                  pl.BlockSpec((1, 1, nck + nlk, 4, L), lambda bi, h: (bi, h, 0, 0, 0)),
                  pl.BlockSpec((1, M_DV), lambda bi, h: (0, h))],
        out_specs=pl.BlockSpec((s, M_DV), lambda bi, h: (bi, h)),
        out_shape=jax.ShapeDtypeStruct((t_lat, M_V_W), BF16),
        scratch_shapes=[pltpu.VMEM((2, M_DQK, M_DV), F32),
                        pltpu.VMEM((2, 1, M_DQK), F32),
                        pltpu.VMEM((2, 1, LANES), F32),
                        pltpu.VMEM((s, M_DV), F32),
                        pltpu.VMEM((s, M_DV), F32)],
        compiler_params=_cparams(("parallel", "parallel")),
        name="mlstm",
    )(p_all, p_all, kt_lat, p_all, p_all, p_all, kt_ctx, p_all, gcol, grow, norm_g.reshape(1, M_V_W))


def _rope(t, cos_t, sin_s, lane_lo):
    swapped = jnp.where(lane_lo, pltpu.roll(t, A_DH - ROPE_HALF, 1), pltpu.roll(t, ROPE_HALF, 1))
    return t * cos_t + swapped * sin_s


def _attn_kernel(sink_ref, q_ref, k_ref, v_ref, kc_ref, vc_ref, cos_ref, sin_ref, out_ref, kpad, vpad):
    s = k_ref.shape[0]
    blk = A_BLOCK
    hk = pl.program_id(1)
    scale = A_DH ** -0.5
    lane = lax.broadcasted_iota(jnp.int32, (1, A_DH), 1)
    lane_lo = (lane % (2 * ROPE_HALF)) < ROPE_HALF

    zero = jnp.zeros((blk, A_DH), BF16)
    kpad[0:blk, :] = zero
    kpad[s + blk:s + 2 * blk, :] = zero
    vpad[0:blk, :] = zero
    vpad[s + blk:s + 2 * blk, :] = zero
    vpad[blk:s + blk, :] = v_ref[...]

    def rope_k(n, carry):
        r0 = pl.multiple_of(n * blk, blk)
        kf = k_ref[pl.ds(r0, blk), :].astype(F32)
        kr = _rope(kf, cos_ref[pl.ds(r0, blk), :], sin_ref[pl.ds(r0, blk), :], lane_lo)
        kpad[pl.ds(r0 + blk, blk), :] = kr.astype(BF16)
        return carry

    lax.fori_loop(0, s // blk, rope_k, 0)

    kc = kc_ref[...]
    vc = vc_ref[...]
    n_ctx = kc.shape[0]
    n_blocks = s // blk
    rr = lax.broadcasted_iota(jnp.int32, (blk, blk), 0)
    cc = lax.broadcasted_iota(jnp.int32, (blk, blk), 1)
    nt = (((1,), (1,)), ((), ()))
    log2e = 1.4426950408889634
    neg = -jnp.inf

    def q_block(n):
        r0 = pl.multiple_of(n * blk, blk)
        cos_q = cos_ref[pl.ds(r0, blk), :]
        sin_q = sin_ref[pl.ds(r0, blk), :]
        kb = kpad[pl.ds(r0, 3 * blk), :]
        vb = vpad[pl.ds(r0, 3 * blk), :]
        bias_prev = jnp.where(cc >= rr, jnp.where(n > 0, 0.0, neg), neg)
        bias_next = jnp.where(cc <= rr, jnp.where(n < n_blocks - 1, 0.0, neg), neg)
        q_all = jnp.concatenate(
            [_rope(q_ref[pl.ds(r0, blk), g * A_DH:(g + 1) * A_DH].astype(F32), cos_q, sin_q,
                   lane_lo).astype(BF16) for g in range(A_GROUP)], axis=0)
        s_loc = lax.dot_general(q_all, kb, nt, preferred_element_type=F32)
        s_ctx = lax.dot_general(q_all, kc, nt, preferred_element_type=F32)
        p_parts, inv_den = [], []
        for g in range(A_GROUP):
            rows = slice(g * blk, (g + 1) * blk)
            tiles = ([s_loc[rows, 0:blk] + bias_prev, s_loc[rows, blk:2 * blk],
                      s_loc[rows, 2 * blk:3 * blk] + bias_next]
                     + [s_ctx[rows, c:c + blk] for c in range(0, n_ctx, blk)])
            tmax = functools.reduce(jnp.maximum, tiles)
            snk = sink_ref[hk * A_GROUP + g]
            m = jnp.maximum(jnp.max(tmax, axis=1, keepdims=True) * scale, snk)
            mb = m * log2e
            p = [jnp.exp2(t * (scale * log2e) - mb) for t in tiles]
            den = jnp.sum(functools.reduce(jnp.add, p), axis=1, keepdims=True) + jnp.exp(snk - m)
            inv_den.append(1.0 / den)
            p_parts.append(jnp.concatenate([t.astype(BF16) for t in p], axis=1))
        p_all = jnp.concatenate(p_parts, axis=0)
        o_all = (jnp.dot(p_all[:, :3 * blk], vb, preferred_element_type=F32)
                 + jnp.dot(p_all[:, 3 * blk:], vc, preferred_element_type=F32))
        for g in range(A_GROUP):
            o = o_all[g * blk:(g + 1) * blk, :] * inv_den[g]
            out_ref[pl.ds(r0, blk), g * A_DH:(g + 1) * A_DH] = o.astype(out_ref.dtype)

    def body(i, carry):
        for u in range(ATTN_UNROLL):
            q_block(i * ATTN_UNROLL + u)
        return carry

    lax.fori_loop(0, n_blocks // ATTN_UNROLL, body, 0)


def _attention(p_all, b, s, n_ctx, sink, cos_t, sin_s):
    gw = A_GROUP * A_DH
    cb = (b * s) // n_ctx
    return pl.pallas_call(
        _attn_kernel,
        grid=(b, A_KV_HEADS),
        in_specs=[pl.BlockSpec(memory_space=pltpu.SMEM),
                  pl.BlockSpec((s, gw), lambda bi, h: (bi, P_QA // gw + h)),
                  pl.BlockSpec((s, A_DH), lambda bi, h: (bi, P_KA // A_DH + h)),
                  pl.BlockSpec((s, A_DH), lambda bi, h: (bi, P_VA // A_DH + h)),
                  pl.BlockSpec((n_ctx, A_DH), lambda bi, h: (cb + bi, P_KA // A_DH + h)),
                  pl.BlockSpec((n_ctx, A_DH), lambda bi, h: (cb + bi, P_VA // A_DH + h)),
                  pl.BlockSpec((s, A_DH), lambda bi, h: (0, 0)),
                  pl.BlockSpec((s, A_DH), lambda bi, h: (0, 0))],
        out_specs=pl.BlockSpec((s, gw), lambda bi, h: (bi, h)),
        out_shape=jax.ShapeDtypeStruct((b * s, A_Q_W), BF16),
        scratch_shapes=[pltpu.VMEM((s + 2 * A_BLOCK, A_DH), BF16),
                        pltpu.VMEM((s + 2 * A_BLOCK, A_DH), BF16)],
        compiler_params=_cparams(("parallel", "parallel")),
        name="attention",
    )(sink, p_all, p_all, p_all, p_all, p_all, cos_t, sin_s)


def _rope_tables(n_tokens):
    rows = n_tokens // GRID_W
    row = jnp.repeat(jnp.arange(rows, dtype=F32), GRID_W)
    col = jnp.tile(jnp.arange(GRID_W, dtype=F32), rows)
    inv_freq = ROPE_BASE ** (-jnp.arange(ROPE_HALF, dtype=F32) / ROPE_HALF)
    ar = row[:, None] * inv_freq
    ac = col[:, None] * inv_freq
    cos_t = jnp.concatenate([jnp.cos(ar), jnp.cos(ar), jnp.cos(ac), jnp.cos(ac)], axis=1)
    sin_s = jnp.concatenate([-jnp.sin(ar), jnp.sin(ar), -jnp.sin(ac), jnp.sin(ac)], axis=1)
    return cos_t, sin_s


def _merge_kernel(hm_ref, at_ref, wm_ref, wa_ref, gm_ref, ga_ref, o_ref):
    bm = jnp.dot(hm_ref[...], wm_ref[...], preferred_element_type=F32)
    ba = jnp.dot(at_ref[...], wa_ref[...], preferred_element_type=F32)
    o = jax.nn.sigmoid(gm_ref[...].astype(F32)) * bm + jax.nn.sigmoid(ga_ref[...].astype(F32)) * ba
    o_ref[...] = o.astype(o_ref.dtype)


def _merge(hm, att, p2d, w_br_m, w_br_a):
    t, d = hm.shape
    tm, tn = 1024, 1024
    assert P_GBM % tn == 0 and P_GBA % tn == 0
    return pl.pallas_call(
        _merge_kernel,
        grid=(t // tm, d // tn),
        in_specs=[pl.BlockSpec((tm, M_V_W), lambda i, j: (i, 0)),
                  pl.BlockSpec((tm, A_Q_W), lambda i, j: (i, 0)),
                  pl.BlockSpec((M_V_W, tn), lambda i, j: (0, j)),
                  pl.BlockSpec((A_Q_W, tn), lambda i, j: (0, j)),
                  pl.BlockSpec((tm, tn), lambda i, j: (i, P_GBM // tn + j)),
                  pl.BlockSpec((tm, tn), lambda i, j: (i, P_GBA // tn + j))],
        out_specs=pl.BlockSpec((tm, tn), lambda i, j: (i, j)),
        out_shape=jax.ShapeDtypeStruct((t, d), BF16),
        compiler_params=_cparams(("parallel", "arbitrary")),
        name="merge",
    )(hm, att, w_br_m, w_br_a, p2d, p2d)


def _outproj_kernel(mg_ref, x_ref, wo_ref, g1_ref, ng_ref, sh_ref, sc_ref, wr_ref, br_ref,
                    x1_ref, h2_ref, ri_ref, rw_ref):
    tm = mg_ref.shape[1]
    for r in range(0, tm, tm // OUTPROJ_SPLIT):
        rows = slice(r, r + tm // OUTPROJ_SPLIT)
        y = jnp.dot(mg_ref[0, rows, :], wo_ref[...], preferred_element_type=F32)
        x1 = x_ref[0, rows, :] + g1_ref[0] * y
        x1_ref[0, rows, :] = x1
        hn = x1 * lax.rsqrt(jnp.mean(x1 * x1, axis=-1, keepdims=True) + EPS) * ng_ref[...]
        h2 = hn * (1.0 + sc_ref[0]) + sh_ref[0]
        h2_ref[0, rows, :] = h2

        logits = jnp.dot(h2.astype(BF16), wr_ref[...], preferred_element_type=F32) + br_ref[...]
        lane = lax.broadcasted_iota(jnp.int32, logits.shape, 1)
        neg = -jnp.inf
        gl = jnp.where(lane < N_GROUPS, logits, neg)
        gmax = jnp.max(gl, axis=1, keepdims=True)
        gsel = jnp.min(jnp.where(gl == gmax, lane, LANES), axis=1, keepdims=True)
        p_sel = 1.0 / jnp.sum(jnp.exp(gl - gmax), axis=1, keepdims=True)
        lo = N_GROUPS + gsel * EXPERTS_PER_GROUP
        el = jnp.where((lane >= lo) & (lane < lo + EXPERTS_PER_GROUP), logits, neg)
        v1 = jnp.max(el, axis=1, keepdims=True)
        i1 = jnp.min(jnp.where(el == v1, lane, LANES), axis=1, keepdims=True)
        el2 = jnp.where(lane == i1, neg, el)
        v2 = jnp.max(el2, axis=1, keepdims=True)
        i2 = jnp.min(jnp.where(el2 == v2, lane, LANES), axis=1, keepdims=True)
        e2 = jnp.exp(v2 - v1)
        w1 = p_sel / (1.0 + e2)
        w2 = p_sel * e2 / (1.0 + e2)
        ri_ref[0, rows, :] = jnp.where(lane == 0, i1 - N_GROUPS, jnp.where(lane == 1, i2 - N_GROUPS, 0))
        rw_ref[0, rows, :] = jnp.where(lane == 0, w1, jnp.where(lane == 1, w2, 0.0))


def _outproj(merged, x, w_out, g1, norm_g, shift, scale, w_r, b_r):
    b, s, d = x.shape
    tm = 512
    row = lambda bi, i: (bi, i, 0)
    per_b = lambda bi, i: (bi, 0, 0)
    fixed = lambda bi, i: (0, 0)
    return pl.pallas_call(
        _outproj_kernel,
        grid=(b, s // tm),
        in_specs=[pl.BlockSpec((1, tm, d), row),
                  pl.BlockSpec((1, tm, d), row),
                  pl.BlockSpec((d, d), fixed),
                  pl.BlockSpec((1, 1, d), per_b),
                  pl.BlockSpec((1, d), fixed),
                  pl.BlockSpec((1, 1, d), per_b),
                  pl.BlockSpec((1, 1, d), per_b),
                  pl.BlockSpec((d, LANES), fixed),
                  pl.BlockSpec((1, LANES), fixed)],
        out_specs=[pl.BlockSpec((1, tm, d), row),
                   pl.BlockSpec((1, tm, d), row),
                   pl.BlockSpec((1, tm, LANES), row),
                   pl.BlockSpec((1, tm, LANES), row)],
        out_shape=[jax.ShapeDtypeStruct((b, s, d), F32),
                   jax.ShapeDtypeStruct((b, s, d), F32),
                   jax.ShapeDtypeStruct((b, s, LANES), jnp.int32),
                   jax.ShapeDtypeStruct((b, s, LANES), F32)],
        compiler_params=_cparams(("parallel", "parallel")),
        name="outproj_router",
    )(merged, x, w_out, g1, norm_g.reshape(1, d), shift, scale, w_r, b_r)


def _moe_kernel(sbe_ref, nsub_ref, base_ref, nused_ref, tok_ref, h_hbm, wg_hbm, wu_hbm, wd_hbm, y_hbm,
                xbuf, xb, ring_gu, ring_d, wb_gu, wb_d, hgu, act, ostage, sem, wsem, osem, *, n_super):
    nused = nused_ref[0]
    d = h_hbm.shape[1]
    kh = d // 2
    fh = D_EXPERT // 2

    def w_copy(c, sb):
        e = sbe_ref[sb]
        if c < 4:
            w_hbm = wg_hbm if c % 2 == 0 else wu_hbm
            return pltpu.make_async_copy(w_hbm.at[0, e, pl.ds((c // 2) * kh, kh), :], ring_gu.at[c],
                                         wsem.at[c])
        return pltpu.make_async_copy(wd_hbm.at[0, e, pl.ds((c - 4) * fh, fh), :], ring_d.at[c - 4],
                                     wsem.at[c])

    def out_copy(sb, p):
        return pltpu.make_async_copy(ostage.at[p], y_hbm.at[pl.ds(sb * MOE_SUPER, MOE_SUPER)], osem.at[p])

    def start_gather(sb, slot):
        base = base_ref[sb]

        def issue(i, carry):
            for u in range(GATHER_UNROLL):
                r = i * GATHER_UNROLL + u
                tok = tok_ref[base + r]
                pltpu.make_async_copy(h_hbm.at[pl.ds(tok, 1)], xbuf.at[slot, pl.ds(r, 1)],
                                      sem.at[slot]).start()
            return carry

        lax.fori_loop(0, nsub_ref[sb] * (MOE_SUB // GATHER_UNROLL), issue, 0)

    def wait_gather(sb, slot):
        def wait_sub(i, carry):
            pltpu.make_async_copy(h_hbm.at[pl.ds(0, MOE_SUB)], xbuf.at[slot, pl.ds(0, MOE_SUB)],
                                  sem.at[slot]).wait()
            return carry

        lax.fori_loop(0, nsub_ref[sb], wait_sub, 0)

    def job_compute(c, n, p):
        def run(m):
            rows = slice(0, m * MOE_SUB)
            if c < 4:
                part = jnp.dot(xb[rows, (c // 2) * kh:(c // 2 + 1) * kh], wb_gu[...],
                               preferred_element_type=F32)
                if c < 2:
                    hgu[c, rows, :] = part
                elif c == 2:
                    hgu[0, rows, :] += part
                else:
                    g = hgu[0, rows, :]
                    act[rows, :] = (g * jax.nn.sigmoid(g) * (hgu[1, rows, :] + part)).astype(BF16)
            else:
                part = jnp.dot(act[rows, (c - 4) * fh:(c - 3) * fh], wb_d[...],
                               preferred_element_type=F32)
                if c == 4:
                    hgu[0, rows, :] = part[:, :kh]
                    hgu[1, rows, :] = part[:, kh:]
                else:
                    y = part + jnp.concatenate([hgu[0, rows, :], hgu[1, rows, :]], axis=1)
                    ostage[p, rows, :] = y.astype(ostage.dtype)

        lax.switch(n - 1, [functools.partial(run, m) for m in range(1, MOE_SUPER // MOE_SUB + 1)])

    ostage[...] = jnp.zeros_like(ostage)
    start_gather(0, 0)
    for c in range(6):
        w_copy(c, 0).start()

    def superblock(s, carry):
        slot = s % 2
        n = nsub_ref[s]
        wait_gather(s, slot)

        @pl.when(s + 1 < nused)
        def _():
            start_gather(s + 1, 1 - slot)

        def cast_x(i, c2):
            rows = pl.ds(pl.multiple_of(i * MOE_SUB, MOE_SUB), MOE_SUB)
            xb[rows, :] = xbuf[slot, rows, :].astype(BF16)
            return c2

        lax.fori_loop(0, n, cast_x, 0)

        for c in range(6):
            w_copy(c, s).wait()
            if c < 4:
                wb_gu[...] = ring_gu[c].astype(BF16)
            else:
                wb_d[...] = ring_d[c - 4].astype(BF16)

            @pl.when(s + 1 < nused)
            def _():
                w_copy(c, s + 1).start()

            if c == 5:
                @pl.when(s >= 2)
                def _():
                    out_copy(s - 2, slot).wait()

            job_compute(c, n, slot)

        out_copy(s, slot).start()
        return carry

    lax.fori_loop(0, nused, superblock, 0)

    @pl.when(nused >= 2)
    def _():
        out_copy(nused - 2, nused % 2).wait()

    out_copy(nused - 1, (nused - 1) % 2).wait()

    ostage[0] = jnp.zeros(ostage.shape[1:], ostage.dtype)

    def zero_start(sb, carry):
        out_copy(sb, 0).start()
        return carry

    def zero_wait(sb, carry):
        out_copy(sb, 0).wait()
        return carry

    lax.fori_loop(nused, n_super, zero_start, 0)
    lax.fori_loop(nused, n_super, zero_wait, 0)


def _moe_experts(h2, sorted_tok, sb_expert, sb_nsub, sb_base, n_used, n_super, w_g, w_u, w_d):
    _, d = h2.shape
    assert d % 2 == 0 and D_EXPERT % 2 == 0 and d // 2 == D_EXPERT
    any_spec = pl.BlockSpec(memory_space=pl.ANY)
    grid_spec = pltpu.PrefetchScalarGridSpec(
        num_scalar_prefetch=5,
        grid=(1,),
        in_specs=[any_spec, any_spec, any_spec, any_spec],
        out_specs=any_spec,
        scratch_shapes=[pltpu.VMEM((2, MOE_SUPER, d), F32),
                        pltpu.VMEM((MOE_SUPER, d), BF16),
                        pltpu.VMEM((4, d // 2, D_EXPERT), F32),
                        pltpu.VMEM((2, D_EXPERT // 2, d), F32),
                        pltpu.VMEM((d // 2, D_EXPERT), BF16),
                        pltpu.VMEM((D_EXPERT // 2, d), BF16),
                        pltpu.VMEM((2, MOE_SUPER, D_EXPERT), F32),
                        pltpu.VMEM((MOE_SUPER, D_EXPERT), BF16),
                        pltpu.VMEM((2, MOE_SUPER, d), BF16),
                        pltpu.SemaphoreType.DMA((2,)),
                        pltpu.SemaphoreType.DMA((6,)),
                        pltpu.SemaphoreType.DMA((2,))],
    )
    return pl.pallas_call(
        functools.partial(_moe_kernel, n_super=n_super),
        grid_spec=grid_spec,
        out_shape=jax.ShapeDtypeStruct((n_super * MOE_SUPER, d), BF16),
        compiler_params=_cparams(("arbitrary",), vmem=MOE_VMEM_LIMIT),
        name="moe_experts",
    )(sb_expert, sb_nsub, sb_base, n_used, sorted_tok, h2, w_g, w_u, w_d)


def _dispatch_tables(eid, n_super):
    flat = eid.reshape(-1)
    n_assign = flat.shape[0]
    order = jnp.argsort(flat).astype(jnp.int32)
    sorted_tok = jnp.concatenate([order // TOP_K, jnp.zeros((MOE_SUB,), jnp.int32)])
    counts = jnp.bincount(flat, length=N_EXPERTS)
    nsup = (counts + MOE_SUPER - 1) // MOE_SUPER
    sup_end = jnp.cumsum(nsup)
    sup_start = sup_end - nsup
    start = jnp.cumsum(counts) - counts
    off = sup_start * MOE_SUPER - start
    d_off = jnp.diff(off, prepend=0)
    pos = jnp.arange(n_assign)
    off_sorted = jnp.sum(jnp.where(pos[:, None] >= start[None, :], d_off[None, :], 0), axis=1)
    dest_sorted = (pos + off_sorted).astype(jnp.int32)
    _, dest = lax.sort((order, dest_sorted), num_keys=1)
    sidx = jnp.arange(n_super)
    sb_expert = jnp.minimum(jnp.sum(sup_end[None, :] <= sidx[:, None], axis=1), N_EXPERTS - 1)
    n_used = sup_end[-1]
    k = sidx - sup_start[sb_expert]
    rows_in = jnp.clip(counts[sb_expert] - k * MOE_SUPER, 0, MOE_SUPER)
    sb_nsub = jnp.where(sidx < n_used, (rows_in + MOE_SUB - 1) // MOE_SUB, 0)
    sb_base = start[sb_expert] + k * MOE_SUPER
    i32 = lambda a: a.astype(jnp.int32)
    return (sorted_tok, dest.reshape(-1, TOP_K), i32(sb_expert), i32(sb_nsub), i32(sb_base),
            i32(n_used).reshape(1))


def _final_kernel(x1_ref, ya_ref, yb_ref, w_ref, g2_ref, fg_ref, o_ref):
    w = w_ref[0]
    moe = w[:, 0:1] * ya_ref[0].astype(F32) + w[:, 1:2] * yb_ref[0].astype(F32)
    x2 = x1_ref[0] + g2_ref[0] * moe
    o_ref[0] = x2 * lax.rsqrt(jnp.mean(x2 * x2, axis=-1, keepdims=True) + EPS) * fg_ref[...]


def _final(x1, ya, yb, w_top, g2, final_g):
    b, s, d = x1.shape
    ts = 512
    row = lambda bi, i: (bi, i, 0)
    return pl.pallas_call(
        _final_kernel,
        grid=(b, s // ts),
        in_specs=[pl.BlockSpec((1, ts, d), row),
                  pl.BlockSpec((1, ts, d), row),
                  pl.BlockSpec((1, ts, d), row),
                  pl.BlockSpec((1, ts, TOP_K), row),
                  pl.BlockSpec((1, 1, d), lambda bi, i: (bi, 0, 0)),
                  pl.BlockSpec((1, d), lambda bi, i: (0, 0))],
        out_specs=pl.BlockSpec((1, ts, d), row),
        out_shape=jax.ShapeDtypeStruct((b, s, d), F32),
        compiler_params=_cparams(("parallel", "parallel")),
        name="final_norm",
    )(x1, ya, yb, w_top, g2, final_g.reshape(1, d))


def kernel(x, c, ctx, c_ctx, w_mod, b_mod, norm1_g, w_in, mlstm_gate_b, mlstm_norm_g, attn_sink, w_br_m, w_br_a, w_out, norm2_g, w_router_grp, b_router_grp, w_router_exp, b_router_exp, w_exp_gate, w_exp_up, w_exp_down, final_norm_g):
    b, s, d = x.shape
    n_ctx = ctx.shape[1]
    assert w_mod.shape[0] == 1, "single-layer block"

    pad_rows = 8 - (b + 1)
    cvec = jnp.concatenate([c, c_ctx[None, :], jnp.zeros((pad_rows, d), F32)], axis=0)
    mods = _adaln(cvec, w_mod[0], b_mod[0])
    g1, sh2, sc2, g2 = [mods[:b, i * d:(i + 1) * d].reshape(b, 1, d) for i in range(2, 6)]
    sh1_all = mods[:, 0:d].reshape(-1, 1, d)
    sc1_all = mods[:, d:2 * d].reshape(-1, 1, d)

    t_lat = b * s
    h_all = _norm_mod(x.reshape(t_lat, d), ctx.reshape(b * n_ctx, d), norm1_g[0], sh1_all, sc1_all, s, b)
    w_in_t = jnp.transpose(w_in[0])
    p_all = _proj(h_all, w_in_t, t_lat)
    b_gate = jnp.pad(mlstm_gate_b[0].astype(F32).reshape(1, GATE_W), ((0, 0), (0, LANES - GATE_W)))
    g_all = _gate_proj(h_all, w_in_t, b_gate)

    g_lat = g_all[:t_lat, :GATE_W].reshape(b, s // M_CHUNK, M_CHUNK, 4, M_HEADS)
    g_ctx = g_all[t_lat:, :GATE_W].reshape(b, n_ctx // M_CHUNK, M_CHUNK, 4, M_HEADS)
    gates = jnp.concatenate([g_ctx, g_lat], axis=1)
    gcol = jnp.transpose(gates, (0, 4, 1, 2, 3))
    grow = jnp.transpose(gates, (0, 4, 1, 3, 2))

    hm = _mlstm(p_all, b, s, n_ctx, gcol, grow, mlstm_norm_g[0])
    cos_t, sin_s = _rope_tables(s)
    att = _attention(p_all, b, s, n_ctx, attn_sink[0], cos_t, sin_s)

    merged = _merge(hm, att, p_all, w_br_m[0].astype(BF16), w_br_a[0].astype(BF16))

    w_r = jnp.concatenate([w_router_grp[0], w_router_exp[0]], axis=1)
    n_r = w_r.shape[1]
    w_r = jnp.pad(w_r, ((0, 0), (0, LANES - n_r))).astype(BF16)
    b_r = jnp.pad(jnp.concatenate([b_router_grp[0], b_router_exp[0]]), (0, LANES - n_r)).reshape(1, LANES)
    x1, h2, r_idx, r_w = _outproj(merged.reshape(b, s, d), x, w_out[0].astype(BF16), g1, norm2_g[0],
                                  sh2, sc2, w_r, b_r)

    t_tok = b * s
    eid = r_idx.reshape(t_tok, LANES)[:, :TOP_K]
    w_top = r_w[:, :, :TOP_K]
    n_super = (t_tok * TOP_K) // MOE_SUPER + N_EXPERTS
    sorted_tok, dest, sb_expert, sb_nsub, sb_base, n_used = _dispatch_tables(eid, n_super)
    ybuf = _moe_experts(h2.reshape(t_tok, d), sorted_tok, sb_expert, sb_nsub, sb_base, n_used, n_super,
                        w_exp_gate, w_exp_up, w_exp_down)
    ya = ybuf.at[dest[:, 0]].get(mode="promise_in_bounds").reshape(b, s, d)
    yb = ybuf.at[dest[:, 1]].get(mode="promise_in_bounds").reshape(b, s, d)

    return _final(x1, ya, yb, w_top, g2, final_norm_g)
```

```python
import functools
import math

import jax
import jax.numpy as jnp
from jax import lax
from jax.experimental import pallas as pl
from jax.experimental.pallas import tpu as pltpu

F32 = jnp.float32
BF16 = jnp.bfloat16

D_MODEL = 2048
EPS = 1e-6
GRID_W = 64

M_HEADS = 8
M_DQK = 128
M_DV = 256
M_CHUNK = 256
assert M_DQK == 128, "mLSTM state rows are laid out one key dim per lane"

A_HEADS = 16
A_KV_HEADS = 4
A_GROUP = A_HEADS // A_KV_HEADS
A_DH = 128
A_BLOCK = 128
WINDOW = 128
ROPE_HALF = A_DH // 4
ROPE_BASE = 10000.0
assert WINDOW == A_BLOCK, "band masks assume the window equals the block size"
ATTN_UNROLL = 4
OUTPROJ_SPLIT = 2

N_GROUPS = 8
EXPERTS_PER_GROUP = 8
N_EXPERTS = N_GROUPS * EXPERTS_PER_GROUP
TOP_K = 2
D_EXPERT = 1024

M_QK_W = M_HEADS * M_DQK
M_V_W = M_HEADS * M_DV
A_Q_W = A_HEADS * A_DH
A_KV_W = A_KV_HEADS * A_DH
GATE_W = 4 * M_HEADS

OFF_QM = 0
OFF_KM = OFF_QM + M_QK_W
OFF_VM = OFF_KM + M_QK_W
OFF_OM = OFF_VM + M_V_W
OFF_GM = OFF_OM + M_V_W
OFF_QA = OFF_GM + GATE_W
OFF_KA = OFF_QA + A_Q_W
OFF_VA = OFF_KA + A_KV_W
OFF_GBR = OFF_VA + A_KV_W
IN_WIDTH = OFF_GBR + 2 * D_MODEL

P_QM = 0
P_KM = P_QM + M_QK_W
P_VM = P_KM + M_QK_W
P_OM = P_VM + M_V_W
P_QA = P_OM + M_V_W
P_KA = P_QA + A_Q_W
P_VA = P_KA + A_KV_W
P_GBM = P_VA + A_KV_W
P_GBA = P_GBM + D_MODEL
P_WIDTH = P_GBA + D_MODEL

LANES = 128
NORM_ROWS = 512
PROJ_TM = 1024
PROJ_TN = 1024
MOE_SUB = 128
MOE_SUPER = 512
GATHER_UNROLL = 8
VMEM_LIMIT = 56 * 1024 * 1024
MOE_VMEM_LIMIT = 60 * 1024 * 1024


def _cparams(sem, vmem=VMEM_LIMIT):
    return pltpu.CompilerParams(dimension_semantics=sem, vmem_limit_bytes=vmem)


def _adaln_kernel(c_ref, w_ref, b_ref, o_ref):
    c = c_ref[...]
    a = (c * jax.nn.sigmoid(c)).astype(BF16)
    o_ref[...] = jnp.dot(a, w_ref[...].astype(BF16), preferred_element_type=F32) + b_ref[...]


def _adaln(cvec, w_mod, b_mod):
    rows, d = cvec.shape
    n = w_mod.shape[1]
    tn = 1024
    return pl.pallas_call(
        _adaln_kernel,
        grid=(n // tn,),
        in_specs=[pl.BlockSpec((rows, d), lambda j: (0, 0)),
                  pl.BlockSpec((d, tn), lambda j: (0, j)),
                  pl.BlockSpec((1, tn), lambda j: (0, j))],
        out_specs=pl.BlockSpec((rows, tn), lambda j: (0, j)),
        out_shape=jax.ShapeDtypeStruct((rows, n), F32),
        compiler_params=_cparams(("parallel",)),
        name="adaln",
    )(cvec, w_mod, b_mod.reshape(1, n))


def _norm_mod_kernel(x_ref, c_ref, g_ref, sh_ref, sc_ref, wg_ref, bg_ref, o_ref, go_ref, *, n_lat_tiles):
    xv = jnp.where(pl.program_id(0) < n_lat_tiles, x_ref[...], c_ref[...])
    y = xv * lax.rsqrt(jnp.mean(xv * xv, axis=-1, keepdims=True) + EPS) * g_ref[...]
    h = (y * (1.0 + sc_ref[0]) + sh_ref[0]).astype(BF16)
    o_ref[...] = h

    wr = lax.broadcasted_iota(jnp.int32, wg_ref.shape, 0)
    wg = jnp.where(wr < GATE_W, wg_ref[...], 0.0).astype(BF16)
    g = lax.dot_general(h, wg, (((1,), (1,)), ((), ())), preferred_element_type=F32) + bg_ref[...]
    for r in range(0, g.shape[0], M_CHUNK):
        go_ref[r:r + M_CHUNK, :] = _gate_preprocess(g[r:r + M_CHUNK, :])


def _norm_mod(x2d, ctx2d, g, shift, scale, rows_per_batch, ctx_mod_row, w_in_t, b_gate):
    t_lat, d = x2d.shape
    t_ctx = ctx2d.shape[0]
    tm = NORM_ROWS
    assert tm % M_CHUNK == 0 and OFF_GM % LANES == 0
    n_lat, n_ctx = t_lat // tm, t_ctx // tm
    tiles_per_batch = rows_per_batch // tm
    mod_map = lambda i: (jnp.where(i < n_lat, i // tiles_per_batch, ctx_mod_row), 0, 0)
    return pl.pallas_call(
        functools.partial(_norm_mod_kernel, n_lat_tiles=n_lat),
        grid=(n_lat + n_ctx,),
        in_specs=[pl.BlockSpec((tm, d), lambda i: (jnp.minimum(i, n_lat - 1), 0)),
                  pl.BlockSpec((tm, d), lambda i: (jnp.maximum(i - n_lat, 0), 0)),
                  pl.BlockSpec((1, d), lambda i: (0, 0)),
                  pl.BlockSpec((1, 1, d), mod_map),
                  pl.BlockSpec((1, 1, d), mod_map),
                  pl.BlockSpec((LANES, d), lambda i: (OFF_GM // LANES, 0)),
                  pl.BlockSpec((1, LANES), lambda i: (0, 0))],
        out_specs=[pl.BlockSpec((tm, d), lambda i: (i, 0)),
                   pl.BlockSpec((tm, LANES), lambda i: (i, 0))],
        out_shape=[jax.ShapeDtypeStruct((t_lat + t_ctx, d), BF16),
                   jax.ShapeDtypeStruct((t_lat + t_ctx, LANES), F32)],
        compiler_params=_cparams(("parallel",)),
        name="norm_mod",
    )(x2d, ctx2d, g.reshape(1, d), shift, scale, w_in_t, b_gate)


def _proj_kernel(h_ref, wa_ref, wb_ref, o_ref, w_bf, *, first_shifted, n_ctx_tiles, ctx_col_tiles):
    j = pl.program_id(0)
    tn = w_bf.shape[0]

    @pl.when(pl.program_id(1) == 0)
    def _():
        @pl.when(j < first_shifted)
        def _():
            w_bf[...] = wa_ref[...].astype(BF16)

        @pl.when(j >= first_shifted)
        def _():
            w_bf[0:tn - GATE_W, :] = wa_ref[GATE_W:tn, :].astype(BF16)
            w_bf[tn - GATE_W:tn, :] = wb_ref[...].astype(BF16)

    needed = pl.program_id(1) >= n_ctx_tiles
    for lo, hi in ctx_col_tiles:
        needed = needed | ((j >= lo) & (j < hi))

    @pl.when(needed)
    def _():
        o_ref[...] = lax.dot_general(h_ref[...], w_bf[...], (((1,), (1,)), ((), ())),
                                     preferred_element_type=F32).astype(o_ref.dtype)

    @pl.when(jnp.logical_not(needed))
    def _():
        o_ref[...] = jnp.zeros_like(o_ref)


def _proj(h_all, w_in_t, t_lat):
    t, d = h_all.shape
    tm, tn = PROJ_TM, PROJ_TN
    assert OFF_GM % tn == 0 and P_WIDTH % tn == 0 and t % tm == 0 and tn % GATE_W == 0 and t_lat % tm == 0
    ctx_col_tiles = ((P_KM // tn, -(-P_OM // tn)), (P_KA // tn, -(-P_GBM // tn)))
    n_tiles, n_lat = t // tm, t_lat // tm
    row_tile = lambda i: (i + n_lat) % n_tiles
    return pl.pallas_call(
        functools.partial(_proj_kernel, first_shifted=OFF_GM // tn, n_ctx_tiles=n_tiles - n_lat,
                          ctx_col_tiles=ctx_col_tiles),
        grid=(P_WIDTH // tn, n_tiles),
        in_specs=[pl.BlockSpec((tm, d), lambda j, i: (row_tile(i), 0)),
                  pl.BlockSpec((tn, d), lambda j, i: (j, 0)),
                  pl.BlockSpec((GATE_W, d), lambda j, i: ((j + 1) * (tn // GATE_W), 0))],
        out_specs=pl.BlockSpec((tm, tn), lambda j, i: (row_tile(i), j)),
        out_shape=jax.ShapeDtypeStruct((t, P_WIDTH), BF16),
        scratch_shapes=[pltpu.VMEM((tn, d), BF16)],
        compiler_params=_cparams(("parallel", "arbitrary")),
        name="in_proj",
    )(h_all, w_in_t, w_in_t)


def _log_sigmoid(x):
    return jnp.minimum(x, 0.0) - jnp.log(1.0 + jnp.exp(-jnp.abs(x)))


def _gate_preprocess(g):
    lf = _log_sigmoid(g)
    L = g.shape[0]
    row = lax.broadcasted_iota(jnp.int32, (L, L), 0)
    col = lax.broadcasted_iota(jnp.int32, (L, L), 1)
    hi = lax.Precision.HIGHEST
    cum_f = jnp.dot((col <= row).astype(F32), lf, precision=hi, preferred_element_type=F32)
    cum_b = jnp.dot((col >= row).astype(F32), lf, precision=hi, preferred_element_type=F32)
    lane = lax.broadcasted_iota(jnp.int32, g.shape, 1)
    is_ff = (lane >= M_HEADS) & (lane < 2 * M_HEADS)
    is_fb = (lane >= 3 * M_HEADS) & (lane < 4 * M_HEADS)
    return jnp.where(is_ff, cum_f, jnp.where(is_fb, cum_b, g))


def _mlstm_chunk(direction, qv, kv, ktv, vv, gcol, grow, state, valid):
    gi = 2 * direction
    igr = grow[gi:gi + 1, :]
    b_r = grow[gi + 1:gi + 2, :]
    L = b_r.shape[1]
    wide = lambda a, n: jnp.concatenate([a] * (n // LANES), axis=1)
    b_end = b_r[:, L - 1:L] if direction == 0 else b_r[:, 0:1]
    kscale = M_DQK ** -0.5

    c_st, n_st, m_st = state

    h = None
    if qv is not None:
        log2e = 1.4426950408889634
        log2k = -0.5 * math.log2(M_DQK)
        row_term = (igr - b_r) * log2e + log2k
        b_c = jnp.broadcast_to(gcol[:, gi + 1:gi + 2], (L, LANES))
        dm2 = jnp.where(valid, wide(b_c * log2e, L) + row_term, -jnp.inf)
        inter = b_c + m_st
        m_t = jnp.maximum(inter, (jnp.max(dm2, axis=1, keepdims=True) - log2k) * (1.0 / log2e))
        w_inter = jnp.exp(inter - m_t)
        sc = lax.dot_general(qv, kv, (((1,), (1,)), ((), ())), preferred_element_type=F32)
        s = sc * jnp.exp2(dm2 - wide(m_t * log2e, L))
        num = (wide(w_inter, M_DV) * jnp.dot(qv, c_st.astype(BF16), preferred_element_type=F32)
               + jnp.dot(s.astype(BF16), vv, preferred_element_type=F32))
        qn = (w_inter * jnp.sum(qv.astype(F32) * n_st, axis=1, keepdims=True)
              + jnp.sum(s, axis=1, keepdims=True))
        h = num * wide(1.0 / jnp.maximum(jnp.abs(qn), jnp.exp(-m_t)), M_DV)

    dec = b_end - b_r + igr
    m_new = jnp.maximum(b_end + m_st, jnp.max(dec, axis=1, keepdims=True))
    w_s = jnp.exp(dec - m_new[:, 0:1]) * kscale
    w_c = jnp.exp(b_end + m_st - m_new)
    wk = (ktv.astype(F32) * w_s).astype(BF16)
    c_new = wide(w_c, M_DV) * c_st + jnp.dot(wk, vv, preferred_element_type=F32)
    n_new = w_c * n_st + jnp.dot(w_s.astype(BF16), kv, preferred_element_type=F32)
    return h, (c_new, n_new, m_new)


def _mlstm_kernel(q_ref, k_ref, kt_ref, v_ref, o_ref, kc_ref, ktc_ref, vc_ref, gcol_ref, grow_ref, ng_ref,
                  out_ref, hf_ref, hb_ref, *, n_ctx_chunks, n_lat_chunks):
    L = M_CHUNK
    row = lax.broadcasted_iota(jnp.int32, (L, L), 0)
    col = lax.broadcasted_iota(jnp.int32, (L, L), 1)
    valid = (col <= row, col >= row)

    zero_state = (jnp.zeros((M_DQK, M_DV), F32), jnp.zeros((1, M_DQK), F32), jnp.zeros((1, LANES), F32))
    states = [zero_state, zero_state]

    for j in range(n_ctx_chunks):
        for direction, jj in ((0, j), (1, n_ctx_chunks - 1 - j)):
            _, states[direction] = _mlstm_chunk(
                direction, None, kc_ref[jj * L:(jj + 1) * L, :], ktc_ref[0, 0, jj],
                vc_ref[jj * L:(jj + 1) * L, :], gcol_ref[0, 0, jj], grow_ref[0, 0, jj],
                states[direction], valid[direction])

    def body(j, carry):
        carry = list(carry)
        for direction, jj, h_ref in ((0, j, hf_ref), (1, n_lat_chunks - 1 - j, hb_ref)):
            r0 = pl.multiple_of(jj * L, L)
            h, carry[direction] = _mlstm_chunk(
                direction, q_ref[pl.ds(r0, L), :], k_ref[pl.ds(r0, L), :], kt_ref[0, 0, jj],
                v_ref[pl.ds(r0, L), :], gcol_ref[0, 0, n_ctx_chunks + jj], grow_ref[0, 0, n_ctx_chunks + jj],
                carry[direction], valid[direction])
            h_ref[pl.ds(r0, L), :] = h
        return tuple(carry)

    lax.fori_loop(0, n_lat_chunks, body, tuple(states))

    def finish(j, carry):
        r0 = pl.multiple_of(j * L, L)
        hh = hf_ref[pl.ds(r0, L), :] + hb_ref[pl.ds(r0, L), :]
        hh = hh * lax.rsqrt(jnp.mean(hh * hh, axis=-1, keepdims=True) + EPS)
        og = jax.nn.sigmoid(o_ref[pl.ds(r0, L), :].astype(F32))
        out_ref[pl.ds(r0, L), :] = (hh * ng_ref[...] * og).astype(out_ref.dtype)
        return carry

    lax.fori_loop(0, n_lat_chunks, finish, 0)


def _chunked_kt(k2d, b, n_chunks):
    k = k2d.reshape(b, n_chunks, M_CHUNK, M_HEADS, M_DQK)
    return jnp.transpose(k, (0, 3, 1, 4, 2))


def _mlstm(p_all, b, s, n_ctx, gcol, grow, norm_g):
    L = M_CHUNK
    nck, nlk = n_ctx // L, s // L
    t_lat = b * s
    cb = t_lat // n_ctx
    kt_lat = _chunked_kt(p_all[:t_lat, P_KM:P_KM + M_QK_W], b, nlk)
    kt_ctx = _chunked_kt(p_all[t_lat:, P_KM:P_KM + M_QK_W], b, nck)
    kern = functools.partial(_mlstm_kernel, n_ctx_chunks=nck, n_lat_chunks=nlk)
    return pl.pallas_call(
        kern,
        grid=(b, M_HEADS),
        in_specs=[pl.BlockSpec((s, M_DQK), lambda bi, h: (bi, P_QM // M_DQK + h)),
                  pl.BlockSpec((s, M_DQK), lambda bi, h: (bi, P_KM // M_DQK + h)),
                  pl.BlockSpec((1, 1, nlk, M_DQK, L), lambda bi, h: (bi, h, 0, 0, 0)),
                  pl.BlockSpec((s, M_DV), lambda bi, h: (bi, P_VM // M_DV + h)),
                  pl.BlockSpec((s, M_DV), lambda bi, h: (bi, P_OM // M_DV + h)),
                  pl.BlockSpec((n_ctx, M_DQK), lambda bi, h: (cb + bi, P_KM // M_DQK + h)),
                  pl.BlockSpec((1, 1, nck, M_DQK, L), lambda bi, h: (bi, h, 0, 0, 0)),
                  pl.BlockSpec((n_ctx, M_DV), lambda bi, h: (cb + bi, P_VM // M_DV + h)),
                  pl.BlockSpec((1, 1, nck + nlk, L, 4), lambda bi, h: (bi, h, 0, 0, 0)),
                  pl.BlockSpec((1, 1, nck + nlk, 4, L), lambda bi, h: (bi, h, 0, 0, 0)),
                  pl.BlockSpec((1, M_DV), lambda bi, h: (0, h))],
        out_specs=pl.BlockSpec((s, M_DV), lambda bi, h: (bi, h)),
        out_shape=jax.ShapeDtypeStruct((t_lat, M_V_W), BF16),
        scratch_shapes=[pltpu.VMEM((s, M_DV), F32),
                        pltpu.VMEM((s, M_DV), F32)],
        compiler_params=_cparams(("parallel", "parallel")),
        name="mlstm",
    )(p_all, p_all, kt_lat, p_all, p_all, p_all, kt_ctx, p_all, gcol, grow, norm_g.reshape(1, M_V_W))


def _rope(t, cos_t, sin_s, lane_lo):
    swapped = jnp.where(lane_lo, pltpu.roll(t, A_DH - ROPE_HALF, 1), pltpu.roll(t, ROPE_HALF, 1))
    return t * cos_t + swapped * sin_s


def _attn_kernel(sink_ref, q_ref, k_ref, v_ref, kc_ref, vc_ref, cos_ref, sin_ref, out_ref, kpad, vpad):
    s = k_ref.shape[0]
    blk = A_BLOCK
    hk = pl.program_id(1)
    scale = A_DH ** -0.5
    lane = lax.broadcasted_iota(jnp.int32, (1, A_DH), 1)
    lane_lo = (lane % (2 * ROPE_HALF)) < ROPE_HALF

    zero = jnp.zeros((blk, A_DH), BF16)
    kpad[0:blk, :] = zero
    kpad[s + blk:s + 2 * blk, :] = zero
    vpad[0:blk, :] = zero
    vpad[s + blk:s + 2 * blk, :] = zero
    vpad[blk:s + blk, :] = v_ref[...]

    def rope_k(n, carry):
        r0 = pl.multiple_of(n * blk, blk)
        kf = k_ref[pl.ds(r0, blk), :].astype(F32)
        kr = _rope(kf, cos_ref[pl.ds(r0, blk), :], sin_ref[pl.ds(r0, blk), :], lane_lo)
        kpad[pl.ds(r0 + blk, blk), :] = kr.astype(BF16)
        return carry

    lax.fori_loop(0, s // blk, rope_k, 0)

    kc = kc_ref[...]
    vc = vc_ref[...]
    n_ctx = kc.shape[0]
    n_blocks = s // blk
    rr = lax.broadcasted_iota(jnp.int32, (blk, blk), 0)
    cc = lax.broadcasted_iota(jnp.int32, (blk, blk), 1)
    nt = (((1,), (1,)), ((), ()))
    log2e = 1.4426950408889634
    neg = -jnp.inf

    def q_block(n):
        r0 = pl.multiple_of(n * blk, blk)
        cos_q = cos_ref[pl.ds(r0, blk), :]
        sin_q = sin_ref[pl.ds(r0, blk), :]
        kb = kpad[pl.ds(r0, 3 * blk), :]
        vb = vpad[pl.ds(r0, 3 * blk), :]
        bias_prev = jnp.where(cc >= rr, jnp.where(n > 0, 0.0, neg), neg)
        bias_next = jnp.where(cc <= rr, jnp.where(n < n_blocks - 1, 0.0, neg), neg)
        q_all = jnp.concatenate(
            [_rope(q_ref[pl.ds(r0, blk), g * A_DH:(g + 1) * A_DH].astype(F32), cos_q, sin_q,
                   lane_lo).astype(BF16) for g in range(A_GROUP)], axis=0)
        s_loc = lax.dot_general(q_all, kb, nt, preferred_element_type=F32)
        s_ctx = lax.dot_general(q_all, kc, nt, preferred_element_type=F32)
        p_parts, inv_den = [], []
        for g in range(A_GROUP):
            rows = slice(g * blk, (g + 1) * blk)
            tiles = ([s_loc[rows, 0:blk] + bias_prev, s_loc[rows, blk:2 * blk],
                      s_loc[rows, 2 * blk:3 * blk] + bias_next]
                     + [s_ctx[rows, c:c + blk] for c in range(0, n_ctx, blk)])
            tmax = functools.reduce(jnp.maximum, tiles)
            snk = sink_ref[hk * A_GROUP + g]
            m = jnp.maximum(jnp.max(tmax, axis=1, keepdims=True) * scale, snk)
            mb = m * log2e
            p = [jnp.exp2(t * (scale * log2e) - mb) for t in tiles]
            den = jnp.sum(functools.reduce(jnp.add, p), axis=1, keepdims=True) + jnp.exp(snk - m)
            inv_den.append(1.0 / den)
            p_parts.append(jnp.concatenate([t.astype(BF16) for t in p], axis=1))
        p_all = jnp.concatenate(p_parts, axis=0)
        o_all = (jnp.dot(p_all[:, :3 * blk], vb, preferred_element_type=F32)
                 + jnp.dot(p_all[:, 3 * blk:], vc, preferred_element_type=F32))
        for g in range(A_GROUP):
            o = o_all[g * blk:(g + 1) * blk, :] * inv_den[g]
            out_ref[pl.ds(r0, blk), g * A_DH:(g + 1) * A_DH] = o.astype(out_ref.dtype)

    def body(i, carry):
        for u in range(ATTN_UNROLL):
            q_block(i * ATTN_UNROLL + u)
        return carry

    lax.fori_loop(0, n_blocks // ATTN_UNROLL, body, 0)


def _attention(p_all, b, s, n_ctx, sink, cos_t, sin_s):
    gw = A_GROUP * A_DH
    cb = (b * s) // n_ctx
    return pl.pallas_call(
        _attn_kernel,
        grid=(b, A_KV_HEADS),
        in_specs=[pl.BlockSpec(memory_space=pltpu.SMEM),
                  pl.BlockSpec((s, gw), lambda bi, h: (bi, P_QA // gw + h)),
                  pl.BlockSpec((s, A_DH), lambda bi, h: (bi, P_KA // A_DH + h)),
                  pl.BlockSpec((s, A_DH), lambda bi, h: (bi, P_VA // A_DH + h)),
                  pl.BlockSpec((n_ctx, A_DH), lambda bi, h: (cb + bi, P_KA // A_DH + h)),
                  pl.BlockSpec((n_ctx, A_DH), lambda bi, h: (cb + bi, P_VA // A_DH + h)),
                  pl.BlockSpec((s, A_DH), lambda bi, h: (0, 0)),
                  pl.BlockSpec((s, A_DH), lambda bi, h: (0, 0))],
        out_specs=pl.BlockSpec((s, gw), lambda bi, h: (bi, h)),
        out_shape=jax.ShapeDtypeStruct((b * s, A_Q_W), BF16),
        scratch_shapes=[pltpu.VMEM((s + 2 * A_BLOCK, A_DH), BF16),
                        pltpu.VMEM((s + 2 * A_BLOCK, A_DH), BF16)],
        compiler_params=_cparams(("parallel", "parallel")),
        name="attention",
    )(sink, p_all, p_all, p_all, p_all, p_all, cos_t, sin_s)


def _rope_tables(n_tokens):
    rows = n_tokens // GRID_W
    row = jnp.repeat(jnp.arange(rows, dtype=F32), GRID_W)
    col = jnp.tile(jnp.arange(GRID_W, dtype=F32), rows)
    inv_freq = ROPE_BASE ** (-jnp.arange(ROPE_HALF, dtype=F32) / ROPE_HALF)
    ar = row[:, None] * inv_freq
    ac = col[:, None] * inv_freq
    cos_t = jnp.concatenate([jnp.cos(ar), jnp.cos(ar), jnp.cos(ac), jnp.cos(ac)], axis=1)
    sin_s = jnp.concatenate([-jnp.sin(ar), jnp.sin(ar), -jnp.sin(ac), jnp.sin(ac)], axis=1)
    return cos_t, sin_s


def _merge_kernel(hm_ref, at_ref, wm_ref, wa_ref, gm_ref, ga_ref, o_ref):
    bm = jnp.dot(hm_ref[...], wm_ref[...], preferred_element_type=F32)
    ba = jnp.dot(at_ref[...], wa_ref[...], preferred_element_type=F32)
    o = jax.nn.sigmoid(gm_ref[...].astype(F32)) * bm + jax.nn.sigmoid(ga_ref[...].astype(F32)) * ba
    o_ref[...] = o.astype(o_ref.dtype)


def _merge(hm, att, p2d, w_br_m, w_br_a):
    t, d = hm.shape
    tm, tn = 1024, 1024
    assert P_GBM % tn == 0 and P_GBA % tn == 0
    return pl.pallas_call(
        _merge_kernel,
        grid=(t // tm, d // tn),
        in_specs=[pl.BlockSpec((tm, M_V_W), lambda i, j: (i, 0)),
                  pl.BlockSpec((tm, A_Q_W), lambda i, j: (i, 0)),
                  pl.BlockSpec((M_V_W, tn), lambda i, j: (0, j)),
                  pl.BlockSpec((A_Q_W, tn), lambda i, j: (0, j)),
                  pl.BlockSpec((tm, tn), lambda i, j: (i, P_GBM // tn + j)),
                  pl.BlockSpec((tm, tn), lambda i, j: (i, P_GBA // tn + j))],
        out_specs=pl.BlockSpec((tm, tn), lambda i, j: (i, j)),
        out_shape=jax.ShapeDtypeStruct((t, d), BF16),
        compiler_params=_cparams(("parallel", "arbitrary")),
        name="merge",
    )(hm, att, w_br_m, w_br_a, p2d, p2d)


def _outproj_kernel(mg_ref, x_ref, wo_ref, g1_ref, ng_ref, sh_ref, sc_ref, wr_ref, br_ref,
                    x1_ref, h2_ref, ri_ref, rw_ref):
    tm = mg_ref.shape[1]
    for r in range(0, tm, tm // OUTPROJ_SPLIT):
        rows = slice(r, r + tm // OUTPROJ_SPLIT)
        y = jnp.dot(mg_ref[0, rows, :], wo_ref[...], preferred_element_type=F32)
        x1 = x_ref[0, rows, :] + g1_ref[0] * y
        x1_ref[0, rows, :] = x1
        hn = x1 * lax.rsqrt(jnp.mean(x1 * x1, axis=-1, keepdims=True) + EPS) * ng_ref[...]
        h2 = hn * (1.0 + sc_ref[0]) + sh_ref[0]
        h2_ref[0, rows, :] = h2

        logits = jnp.dot(h2.astype(BF16), wr_ref[...], preferred_element_type=F32) + br_ref[...]
        lane = lax.broadcasted_iota(jnp.int32, logits.shape, 1)
        neg = -jnp.inf
        gl = jnp.where(lane < N_GROUPS, logits, neg)
        gmax = jnp.max(gl, axis=1, keepdims=True)
        gsel = jnp.min(jnp.where(gl == gmax, lane, LANES), axis=1, keepdims=True)
        p_sel = 1.0 / jnp.sum(jnp.exp(gl - gmax), axis=1, keepdims=True)
        lo = N_GROUPS + gsel * EXPERTS_PER_GROUP
        el = jnp.where((lane >= lo) & (lane < lo + EXPERTS_PER_GROUP), logits, neg)
        v1 = jnp.max(el, axis=1, keepdims=True)
        i1 = jnp.min(jnp.where(el == v1, lane, LANES), axis=1, keepdims=True)
        el2 = jnp.where(lane == i1, neg, el)
        v2 = jnp.max(el2, axis=1, keepdims=True)
        i2 = jnp.min(jnp.where(el2 == v2, lane, LANES), axis=1, keepdims=True)
        e2 = jnp.exp(v2 - v1)
        w1 = p_sel / (1.0 + e2)
        w2 = p_sel * e2 / (1.0 + e2)
        ri_ref[0, rows, :] = jnp.where(lane == 0, i1 - N_GROUPS, jnp.where(lane == 1, i2 - N_GROUPS, 0))
        rw_ref[0, rows, :] = jnp.where(lane == 0, w1, jnp.where(lane == 1, w2, 0.0))


def _outproj(merged, x, w_out, g1, norm_g, shift, scale, w_r, b_r):
    b, s, d = x.shape
    tm = 512
    row = lambda bi, i: (bi, i, 0)
    per_b = lambda bi, i: (bi, 0, 0)
    fixed = lambda bi, i: (0, 0)
    return pl.pallas_call(
        _outproj_kernel,
        grid=(b, s // tm),
        in_specs=[pl.BlockSpec((1, tm, d), row),
                  pl.BlockSpec((1, tm, d), row),
                  pl.BlockSpec((d, d), fixed),
                  pl.BlockSpec((1, 1, d), per_b),
                  pl.BlockSpec((1, d), fixed),
                  pl.BlockSpec((1, 1, d), per_b),
                  pl.BlockSpec((1, 1, d), per_b),
                  pl.BlockSpec((d, LANES), fixed),
                  pl.BlockSpec((1, LANES), fixed)],
        out_specs=[pl.BlockSpec((1, tm, d), row),
                   pl.BlockSpec((1, tm, d), row),
                   pl.BlockSpec((1, tm, LANES), row),
                   pl.BlockSpec((1, tm, LANES), row)],
        out_shape=[jax.ShapeDtypeStruct((b, s, d), F32),
                   jax.ShapeDtypeStruct((b, s, d), F32),
                   jax.ShapeDtypeStruct((b, s, LANES), jnp.int32),
                   jax.ShapeDtypeStruct((b, s, LANES), F32)],
        compiler_params=_cparams(("parallel", "parallel")),
        name="outproj_router",
    )(merged, x, w_out, g1, norm_g.reshape(1, d), shift, scale, w_r, b_r)


def _moe_kernel(sbe_ref, nsub_ref, base_ref, nused_ref, tok_ref, h_hbm, wg_hbm, wu_hbm, wd_hbm, y_hbm,
                xbuf, xb, ring_gu, ring_d, wb_gu, wb_d, hgu, act, ostage, sem, wsem, osem, *, n_super):
    nused = nused_ref[0]
    d = h_hbm.shape[1]
    kh = d // 2
    fh = D_EXPERT // 2

    def w_copy(c, sb):
        e = sbe_ref[sb]
        if c < 4:
            w_hbm = wg_hbm if c % 2 == 0 else wu_hbm
            return pltpu.make_async_copy(w_hbm.at[0, e, pl.ds((c // 2) * kh, kh), :], ring_gu.at[c],
                                         wsem.at[c])
        return pltpu.make_async_copy(wd_hbm.at[0, e, pl.ds((c - 4) * fh, fh), :], ring_d.at[c - 4],
                                     wsem.at[c])

    def out_copy(sb, p):
        return pltpu.make_async_copy(ostage.at[p], y_hbm.at[pl.ds(sb * MOE_SUPER, MOE_SUPER)], osem.at[p])

    def start_gather(sb, slot):
        base = base_ref[sb]

        def issue(i, carry):
            for u in range(GATHER_UNROLL):
                r = i * GATHER_UNROLL + u
                tok = tok_ref[base + r]
                pltpu.make_async_copy(h_hbm.at[pl.ds(tok, 1)], xbuf.at[slot, pl.ds(r, 1)],
                                      sem.at[slot]).start()
            return carry

        lax.fori_loop(0, nsub_ref[sb] * (MOE_SUB // GATHER_UNROLL), issue, 0)

    def wait_gather(sb, slot):
        def wait_sub(i, carry):
            pltpu.make_async_copy(h_hbm.at[pl.ds(0, MOE_SUB)], xbuf.at[slot, pl.ds(0, MOE_SUB)],
                                  sem.at[slot]).wait()
            return carry

        lax.fori_loop(0, nsub_ref[sb], wait_sub, 0)

    def job_compute(c, n, p):
        def run(m):
            rows = slice(0, m * MOE_SUB)
            if c < 4:
                part = jnp.dot(xb[rows, (c // 2) * kh:(c // 2 + 1) * kh], wb_gu[...],
                               preferred_element_type=F32)
                if c < 2:
                    hgu[c, rows, :] = part
                elif c == 2:
                    hgu[0, rows, :] += part
                else:
                    g = hgu[0, rows, :]
                    act[rows, :] = (g * jax.nn.sigmoid(g) * (hgu[1, rows, :] + part)).astype(BF16)
            else:
                part = jnp.dot(act[rows, (c - 4) * fh:(c - 3) * fh], wb_d[...],
                               preferred_element_type=F32)
                if c == 4:
                    hgu[0, rows, :] = part[:, :kh]
                    hgu[1, rows, :] = part[:, kh:]
                else:
                    y = part + jnp.concatenate([hgu[0, rows, :], hgu[1, rows, :]], axis=1)
                    ostage[p, rows, :] = y.astype(ostage.dtype)

        lax.switch(n - 1, [functools.partial(run, m) for m in range(1, MOE_SUPER // MOE_SUB + 1)])

    ostage[...] = jnp.zeros_like(ostage)
    start_gather(0, 0)
    for c in range(6):
        w_copy(c, 0).start()

    def superblock(s, carry):
        slot = s % 2
        n = nsub_ref[s]
        wait_gather(s, slot)

        @pl.when(s + 1 < nused)
        def _():
            start_gather(s + 1, 1 - slot)

        def cast_x(i, c2):
            rows = pl.ds(pl.multiple_of(i * MOE_SUB, MOE_SUB), MOE_SUB)
            xb[rows, :] = xbuf[slot, rows, :].astype(BF16)
            return c2

        lax.fori_loop(0, n, cast_x, 0)

        for c in range(6):
            w_copy(c, s).wait()
            if c < 4:
                wb_gu[...] = ring_gu[c].astype(BF16)
            else:
                wb_d[...] = ring_d[c - 4].astype(BF16)

            @pl.when(s + 1 < nused)
            def _():
                w_copy(c, s + 1).start()

            if c == 5:
                @pl.when(s >= 2)
                def _():
                    out_copy(s - 2, slot).wait()

            job_compute(c, n, slot)

        out_copy(s, slot).start()
        return carry

    lax.fori_loop(0, nused, superblock, 0)

    @pl.when(nused >= 2)
    def _():
        out_copy(nused - 2, nused % 2).wait()

    out_copy(nused - 1, (nused - 1) % 2).wait()

    ostage[0] = jnp.zeros(ostage.shape[1:], ostage.dtype)

    def zero_start(sb, carry):
        out_copy(sb, 0).start()
        return carry

    def zero_wait(sb, carry):
        out_copy(sb, 0).wait()
        return carry

    lax.fori_loop(nused, n_super, zero_start, 0)
    lax.fori_loop(nused, n_super, zero_wait, 0)


def _moe_experts(h2, sorted_tok, sb_expert, sb_nsub, sb_base, n_used, n_super, w_g, w_u, w_d):
    _, d = h2.shape
    assert d % 2 == 0 and D_EXPERT % 2 == 0 and d // 2 == D_EXPERT
    any_spec = pl.BlockSpec(memory_space=pl.ANY)
    grid_spec = pltpu.PrefetchScalarGridSpec(
        num_scalar_prefetch=5,
        grid=(1,),
        in_specs=[any_spec, any_spec, any_spec, any_spec],
        out_specs=any_spec,
        scratch_shapes=[pltpu.VMEM((2, MOE_SUPER, d), F32),
                        pltpu.VMEM((MOE_SUPER, d), BF16),
                        pltpu.VMEM((4, d // 2, D_EXPERT), F32),
                        pltpu.VMEM((2, D_EXPERT // 2, d), F32),
                        pltpu.VMEM((d // 2, D_EXPERT), BF16),
                        pltpu.VMEM((D_EXPERT // 2, d), BF16),
                        pltpu.VMEM((2, MOE_SUPER, D_EXPERT), F32),
                        pltpu.VMEM((MOE_SUPER, D_EXPERT), BF16),
                        pltpu.VMEM((2, MOE_SUPER, d), BF16),
                        pltpu.SemaphoreType.DMA((2,)),
                        pltpu.SemaphoreType.DMA((6,)),
                        pltpu.SemaphoreType.DMA((2,))],
    )
    return pl.pallas_call(
        functools.partial(_moe_kernel, n_super=n_super),
        grid_spec=grid_spec,
        out_shape=jax.ShapeDtypeStruct((n_super * MOE_SUPER, d), BF16),
        compiler_params=_cparams(("arbitrary",), vmem=MOE_VMEM_LIMIT),
        name="moe_experts",
    )(sb_expert, sb_nsub, sb_base, n_used, sorted_tok, h2, w_g, w_u, w_d)


def _dispatch_tables(eid, n_super):
    flat = eid.reshape(-1)
    n_assign = flat.shape[0]
    order = jnp.argsort(flat).astype(jnp.int32)
    sorted_tok = jnp.concatenate([order // TOP_K, jnp.zeros((MOE_SUB,), jnp.int32)])
    counts = jnp.bincount(flat, length=N_EXPERTS)
    nsup = (counts + MOE_SUPER - 1) // MOE_SUPER
    sup_end = jnp.cumsum(nsup)
    sup_start = sup_end - nsup
    start = jnp.cumsum(counts) - counts
    off = sup_start * MOE_SUPER - start
    d_off = jnp.diff(off, prepend=0)
    pos = jnp.arange(n_assign)
    off_sorted = jnp.sum(jnp.where(pos[:, None] >= start[None, :], d_off[None, :], 0), axis=1)
    dest_sorted = (pos + off_sorted).astype(jnp.int32)
    _, dest = lax.sort((order, dest_sorted), num_keys=1)
    sidx = jnp.arange(n_super)
    sb_expert = jnp.minimum(jnp.sum(sup_end[None, :] <= sidx[:, None], axis=1), N_EXPERTS - 1)
    n_used = sup_end[-1]
    k = sidx - sup_start[sb_expert]
    rows_in = jnp.clip(counts[sb_expert] - k * MOE_SUPER, 0, MOE_SUPER)
    sb_nsub = jnp.where(sidx < n_used, (rows_in + MOE_SUB - 1) // MOE_SUB, 0)
    sb_base = start[sb_expert] + k * MOE_SUPER
    i32 = lambda a: a.astype(jnp.int32)
    return (sorted_tok, dest.reshape(-1, TOP_K), i32(sb_expert), i32(sb_nsub), i32(sb_base),
            i32(n_used).reshape(1))


def _final_kernel(x1_ref, ya_ref, yb_ref, w_ref, g2_ref, fg_ref, o_ref):
    w = w_ref[0]
    moe = w[:, 0:1] * ya_ref[0].astype(F32) + w[:, 1:2] * yb_ref[0].astype(F32)
    x2 = x1_ref[0] + g2_ref[0] * moe
    o_ref[0] = x2 * lax.rsqrt(jnp.mean(x2 * x2, axis=-1, keepdims=True) + EPS) * fg_ref[...]


def _final(x1, ya, yb, w_top, g2, final_g):
    b, s, d = x1.shape
    ts = 512
    row = lambda bi, i: (bi, i, 0)
    return pl.pallas_call(
        _final_kernel,
        grid=(b, s // ts),
        in_specs=[pl.BlockSpec((1, ts, d), row),
                  pl.BlockSpec((1, ts, d), row),
                  pl.BlockSpec((1, ts, d), row),
                  pl.BlockSpec((1, ts, TOP_K), row),
                  pl.BlockSpec((1, 1, d), lambda bi, i: (bi, 0, 0)),
                  pl.BlockSpec((1, d), lambda bi, i: (0, 0))],
        out_specs=pl.BlockSpec((1, ts, d), row),
        out_shape=jax.ShapeDtypeStruct((b, s, d), F32),
        compiler_params=_cparams(("parallel", "parallel")),
        name="final_norm",
    )(x1, ya, yb, w_top, g2, final_g.reshape(1, d))


def kernel(x, c, ctx, c_ctx, w_mod, b_mod, norm1_g, w_in, mlstm_gate_b, mlstm_norm_g, attn_sink, w_br_m, w_br_a, w_out, norm2_g, w_router_grp, b_router_grp, w_router_exp, b_router_exp, w_exp_gate, w_exp_up, w_exp_down, final_norm_g):
    b, s, d = x.shape
    n_ctx = ctx.shape[1]
    assert w_mod.shape[0] == 1, "single-layer block"

    pad_rows = 8 - (b + 1)
    cvec = jnp.concatenate([c, c_ctx[None, :], jnp.zeros((pad_rows, d), F32)], axis=0)
    mods = _adaln(cvec, w_mod[0], b_mod[0])
    g1, sh2, sc2, g2 = [mods[:b, i * d:(i + 1) * d].reshape(b, 1, d) for i in range(2, 6)]
    sh1_all = mods[:, 0:d].reshape(-1, 1, d)
    sc1_all = mods[:, d:2 * d].reshape(-1, 1, d)

    t_lat = b * s
    w_in_t = jnp.transpose(w_in[0])
    b_gate = jnp.pad(mlstm_gate_b[0].astype(F32).reshape(1, GATE_W), ((0, 0), (0, LANES - GATE_W)))
    h_all, g_all = _norm_mod(x.reshape(t_lat, d), ctx.reshape(b * n_ctx, d), norm1_g[0], sh1_all, sc1_all,
                             s, b, w_in_t, b_gate)
    p_all = _proj(h_all, w_in_t, t_lat)

    g_lat = g_all[:t_lat, :GATE_W].reshape(b, s // M_CHUNK, M_CHUNK, 4, M_HEADS)
    g_ctx = g_all[t_lat:, :GATE_W].reshape(b, n_ctx // M_CHUNK, M_CHUNK, 4, M_HEADS)
    gates = jnp.concatenate([g_ctx, g_lat], axis=1)
    gcol = jnp.transpose(gates, (0, 4, 1, 2, 3))
    grow = jnp.transpose(gates, (0, 4, 1, 3, 2))

    hm = _mlstm(p_all, b, s, n_ctx, gcol, grow, mlstm_norm_g[0])
    cos_t, sin_s = _rope_tables(s)
    att = _attention(p_all, b, s, n_ctx, attn_sink[0], cos_t, sin_s)

    merged = _merge(hm, att, p_all, w_br_m[0].astype(BF16), w_br_a[0].astype(BF16))

    w_r = jnp.concatenate([w_router_grp[0], w_router_exp[0]], axis=1)
    n_r = w_r.shape[1]
    w_r = jnp.pad(w_r, ((0, 0), (0, LANES - n_r))).astype(BF16)
    b_r = jnp.pad(jnp.concatenate([b_router_grp[0], b_router_exp[0]]), (0, LANES - n_r)).reshape(1, LANES)
    x1, h2, r_idx, r_w = _outproj(merged.reshape(b, s, d), x, w_out[0].astype(BF16), g1, norm2_g[0],
                                  sh2, sc2, w_r, b_r)

    t_tok = b * s
    eid = r_idx.reshape(t_tok, LANES)[:, :TOP_K]
    w_top = r_w[:, :, :TOP_K]
    n_super = (t_tok * TOP_K) // MOE_SUPER + N_EXPERTS
    sorted_tok, dest, sb_expert, sb_nsub, sb_base, n_used = _dispatch_tables(eid, n_super)
    ybuf = _moe_experts(h2.reshape(t_tok, d), sorted_tok, sb_expert, sb_nsub, sb_base, n_used, n_super,
                        w_exp_gate, w_exp_up, w_exp_down)
    ya = ybuf.at[dest[:, 0]].get(mode="promise_in_bounds").reshape(b, s, d)
    yb = ybuf.at[dest[:, 1]].get(mode="promise_in_bounds").reshape(b, s, d)

    return _final(x1, ya, yb, w_top, g2, final_norm_g)
```

```python
import functools
import math

import jax
import jax.numpy as jnp
from jax import lax
from jax.experimental import pallas as pl
from jax.experimental.pallas import tpu as pltpu

F32 = jnp.float32
BF16 = jnp.bfloat16

D_MODEL = 2048
EPS = 1e-6
GRID_W = 64

M_HEADS = 8
M_DQK = 128
M_DV = 256
M_CHUNK = 256
assert M_DQK == 128, "mLSTM state rows are laid out one key dim per lane"

A_HEADS = 16
A_KV_HEADS = 4
A_GROUP = A_HEADS // A_KV_HEADS
A_DH = 128
A_BLOCK = 128
WINDOW = 128
ROPE_HALF = A_DH // 4
ROPE_BASE = 10000.0
assert WINDOW == A_BLOCK, "band masks assume the window equals the block size"
ATTN_UNROLL = 4
OUTPROJ_SPLIT = 2

N_GROUPS = 8
EXPERTS_PER_GROUP = 8
N_EXPERTS = N_GROUPS * EXPERTS_PER_GROUP
TOP_K = 2
D_EXPERT = 1024

M_QK_W = M_HEADS * M_DQK
M_V_W = M_HEADS * M_DV
A_Q_W = A_HEADS * A_DH
A_KV_W = A_KV_HEADS * A_DH
GATE_W = 4 * M_HEADS

OFF_QM = 0
OFF_KM = OFF_QM + M_QK_W
OFF_VM = OFF_KM + M_QK_W
OFF_OM = OFF_VM + M_V_W
OFF_GM = OFF_OM + M_V_W
OFF_QA = OFF_GM + GATE_W
OFF_KA = OFF_QA + A_Q_W
OFF_VA = OFF_KA + A_KV_W
OFF_GBR = OFF_VA + A_KV_W
IN_WIDTH = OFF_GBR + 2 * D_MODEL

P_QM = 0
P_KM = P_QM + M_QK_W
P_VM = P_KM + M_QK_W
P_OM = P_VM + M_V_W
P_QA = P_OM + M_V_W
P_KA = P_QA + A_Q_W
P_VA = P_KA + A_KV_W
P_GBM = P_VA + A_KV_W
P_GBA = P_GBM + D_MODEL
P_WIDTH = P_GBA + D_MODEL

LANES = 128
NORM_ROWS = 512
PROJ_TM = 1024
PROJ_TN = 1024
MOE_SUB = 128
MOE_SUPER = 512
GATHER_UNROLL = 8
VMEM_LIMIT = 56 * 1024 * 1024
MOE_VMEM_LIMIT = 60 * 1024 * 1024


def _cparams(sem, vmem=VMEM_LIMIT):
    return pltpu.CompilerParams(dimension_semantics=sem, vmem_limit_bytes=vmem)


def _adaln_kernel(c_ref, w_ref, b_ref, o_ref):
    c = c_ref[...]
    a = (c * jax.nn.sigmoid(c)).astype(BF16)
    o_ref[...] = jnp.dot(a, w_ref[...].astype(BF16), preferred_element_type=F32) + b_ref[...]


def _adaln(cvec, w_mod, b_mod):
    rows, d = cvec.shape
    n = w_mod.shape[1]
    tn = 1024
    return pl.pallas_call(
        _adaln_kernel,
        grid=(n // tn,),
        in_specs=[pl.BlockSpec((rows, d), lambda j: (0, 0)),
                  pl.BlockSpec((d, tn), lambda j: (0, j)),
                  pl.BlockSpec((1, tn), lambda j: (0, j))],
        out_specs=pl.BlockSpec((rows, tn), lambda j: (0, j)),
        out_shape=jax.ShapeDtypeStruct((rows, n), F32),
        compiler_params=_cparams(("parallel",)),
        name="adaln",
    )(cvec, w_mod, b_mod.reshape(1, n))


def _norm_mod_kernel(x_ref, c_ref, g_ref, sh_ref, sc_ref, wg_ref, bg_ref, o_ref, go_ref, *, n_lat_tiles):
    xv = jnp.where(pl.program_id(0) < n_lat_tiles, x_ref[...], c_ref[...])
    y = xv * lax.rsqrt(jnp.mean(xv * xv, axis=-1, keepdims=True) + EPS) * g_ref[...]
    h = (y * (1.0 + sc_ref[0]) + sh_ref[0]).astype(BF16)
    o_ref[...] = h

    wr = lax.broadcasted_iota(jnp.int32, wg_ref.shape, 0)
    wg = jnp.where(wr < GATE_W, wg_ref[...], 0.0).astype(BF16)
    g = lax.dot_general(h, wg, (((1,), (1,)), ((), ())), preferred_element_type=F32) + bg_ref[...]
    for r in range(0, g.shape[0], M_CHUNK):
        go_ref[r:r + M_CHUNK, :] = _gate_preprocess(g[r:r + M_CHUNK, :])


def _norm_mod(x2d, ctx2d, g, shift, scale, rows_per_batch, ctx_mod_row, w_in_t, b_gate):
    t_lat, d = x2d.shape
    t_ctx = ctx2d.shape[0]
    tm = NORM_ROWS
    assert tm % M_CHUNK == 0 and OFF_GM % LANES == 0
    n_lat, n_ctx = t_lat // tm, t_ctx // tm
    tiles_per_batch = rows_per_batch // tm
    mod_map = lambda i: (jnp.where(i < n_lat, i // tiles_per_batch, ctx_mod_row), 0, 0)
    return pl.pallas_call(
        functools.partial(_norm_mod_kernel, n_lat_tiles=n_lat),
        grid=(n_lat + n_ctx,),
        in_specs=[pl.BlockSpec((tm, d), lambda i: (jnp.minimum(i, n_lat - 1), 0)),
                  pl.BlockSpec((tm, d), lambda i: (jnp.maximum(i - n_lat, 0), 0)),
                  pl.BlockSpec((1, d), lambda i: (0, 0)),
                  pl.BlockSpec((1, 1, d), mod_map),
                  pl.BlockSpec((1, 1, d), mod_map),
                  pl.BlockSpec((LANES, d), lambda i: (OFF_GM // LANES, 0)),
                  pl.BlockSpec((1, LANES), lambda i: (0, 0))],
        out_specs=[pl.BlockSpec((tm, d), lambda i: (i, 0)),
                   pl.BlockSpec((tm, LANES), lambda i: (i, 0))],
        out_shape=[jax.ShapeDtypeStruct((t_lat + t_ctx, d), BF16),
                   jax.ShapeDtypeStruct((t_lat + t_ctx, LANES), F32)],
        compiler_params=_cparams(("parallel",)),
        name="norm_mod",
    )(x2d, ctx2d, g.reshape(1, d), shift, scale, w_in_t, b_gate)


def _proj_kernel(h_ref, wa_ref, wb_ref, o_ref, w_bf, *, first_shifted, n_ctx_tiles, ctx_col_tiles):
    j = pl.program_id(0)
    tn = w_bf.shape[0]

    @pl.when(pl.program_id(1) == 0)
    def _():
        @pl.when(j < first_shifted)
        def _():
            w_bf[...] = wa_ref[...].astype(BF16)

        @pl.when(j >= first_shifted)
        def _():
            w_bf[0:tn - GATE_W, :] = wa_ref[GATE_W:tn, :].astype(BF16)
            w_bf[tn - GATE_W:tn, :] = wb_ref[...].astype(BF16)

    needed = pl.program_id(1) >= n_ctx_tiles
    for lo, hi in ctx_col_tiles:
        needed = needed | ((j >= lo) & (j < hi))

    @pl.when(needed)
    def _():
        o_ref[...] = lax.dot_general(h_ref[...], w_bf[...], (((1,), (1,)), ((), ())),
                                     preferred_element_type=F32).astype(o_ref.dtype)

    @pl.when(jnp.logical_not(needed))
    def _():
        o_ref[...] = jnp.zeros_like(o_ref)


def _proj(h_all, w_in_t, t_lat):
    t, d = h_all.shape
    tm, tn = PROJ_TM, PROJ_TN
    assert OFF_GM % tn == 0 and P_WIDTH % tn == 0 and t % tm == 0 and tn % GATE_W == 0 and t_lat % tm == 0
    ctx_col_tiles = ((P_KM // tn, -(-P_OM // tn)), (P_KA // tn, -(-P_GBM // tn)))
    n_tiles, n_lat = t // tm, t_lat // tm
    row_tile = lambda i: (i + n_lat) % n_tiles
    return pl.pallas_call(
        functools.partial(_proj_kernel, first_shifted=OFF_GM // tn, n_ctx_tiles=n_tiles - n_lat,
                          ctx_col_tiles=ctx_col_tiles),
        grid=(P_WIDTH // tn, n_tiles),
        in_specs=[pl.BlockSpec((tm, d), lambda j, i: (row_tile(i), 0)),
                  pl.BlockSpec((tn, d), lambda j, i: (j, 0)),
                  pl.BlockSpec((GATE_W, d), lambda j, i: ((j + 1) * (tn // GATE_W), 0))],
        out_specs=pl.BlockSpec((tm, tn), lambda j, i: (row_tile(i), j)),
        out_shape=jax.ShapeDtypeStruct((t, P_WIDTH), BF16),
        scratch_shapes=[pltpu.VMEM((tn, d), BF16)],
        compiler_params=_cparams(("parallel", "arbitrary")),
        name="in_proj",
    )(h_all, w_in_t, w_in_t)


def _log_sigmoid(x):
    return jnp.minimum(x, 0.0) - jnp.log(1.0 + jnp.exp(-jnp.abs(x)))


def _gate_preprocess(g):
    lf = _log_sigmoid(g)
    L = g.shape[0]
    row = lax.broadcasted_iota(jnp.int32, (L, L), 0)
    col = lax.broadcasted_iota(jnp.int32, (L, L), 1)
    hi = lax.Precision.HIGHEST
    cum_f = jnp.dot((col <= row).astype(F32), lf, precision=hi, preferred_element_type=F32)
    cum_b = jnp.dot((col >= row).astype(F32), lf, precision=hi, preferred_element_type=F32)
    lane = lax.broadcasted_iota(jnp.int32, g.shape, 1)
    is_ff = (lane >= M_HEADS) & (lane < 2 * M_HEADS)
    is_fb = (lane >= 3 * M_HEADS) & (lane < 4 * M_HEADS)
    return jnp.where(is_ff, cum_f, jnp.where(is_fb, cum_b, g))


def _mlstm_chunk(direction, head, qv, kv, ktv, vv, gtok, grow, state, valid):
    gi = 2 * direction
    igr = grow[gi:gi + 1, :]
    b_r = grow[gi + 1:gi + 2, :]
    L = b_r.shape[1]
    wide = lambda a, n: jnp.concatenate([a] * (n // LANES), axis=1)
    b_end = b_r[:, L - 1:L] if direction == 0 else b_r[:, 0:1]
    kscale = M_DQK ** -0.5

    c_st, n_st, m_st = state

    h = None
    if qv is not None:
        log2e = 1.4426950408889634
        log2k = -0.5 * math.log2(M_DQK)
        row_term = (igr - b_r) * log2e + log2k
        tok_lane = lax.broadcasted_iota(jnp.int32, gtok.shape, 1)
        b_c = jnp.broadcast_to(
            jnp.sum(jnp.where(tok_lane == (gi + 1) * M_HEADS + head, gtok, 0.0), axis=1, keepdims=True),
            (L, LANES))
        dm2 = jnp.where(valid, wide(b_c * log2e, L) + row_term, -jnp.inf)
        inter = b_c + m_st
        m_t = jnp.maximum(inter, (jnp.max(dm2, axis=1, keepdims=True) - log2k) * (1.0 / log2e))
        w_inter = jnp.exp(inter - m_t)
        sc = lax.dot_general(qv, kv, (((1,), (1,)), ((), ())), preferred_element_type=F32)
        s = sc * jnp.exp2(dm2 - wide(m_t * log2e, L))
        num = (wide(w_inter, M_DV) * jnp.dot(qv, c_st.astype(BF16), preferred_element_type=F32)
               + jnp.dot(s.astype(BF16), vv, preferred_element_type=F32))
        qn = (w_inter * jnp.sum(qv.astype(F32) * n_st, axis=1, keepdims=True)
              + jnp.sum(s, axis=1, keepdims=True))
        h = num * wide(1.0 / jnp.maximum(jnp.abs(qn), jnp.exp(-m_t)), M_DV)

    dec = b_end - b_r + igr
    m_new = jnp.maximum(b_end + m_st, jnp.max(dec, axis=1, keepdims=True))
    w_s = jnp.exp(dec - m_new[:, 0:1]) * kscale
    w_c = jnp.exp(b_end + m_st - m_new)
    wk = (ktv.astype(F32) * w_s).astype(BF16)
    c_new = wide(w_c, M_DV) * c_st + jnp.dot(wk, vv, preferred_element_type=F32)
    n_new = w_c * n_st + jnp.dot(w_s.astype(BF16), kv, preferred_element_type=F32)
    return h, (c_new, n_new, m_new)


def _mlstm_kernel(q_ref, k_ref, kt_ref, v_ref, o_ref, kc_ref, ktc_ref, vc_ref, gtok_ref, grow_ref, ng_ref,
                  out_ref, hf_ref, hb_ref, *, n_ctx_chunks, n_lat_chunks):
    L = M_CHUNK
    head = pl.program_id(1)
    row = lax.broadcasted_iota(jnp.int32, (L, L), 0)
    col = lax.broadcasted_iota(jnp.int32, (L, L), 1)
    valid = (col <= row, col >= row)

    zero_state = (jnp.zeros((M_DQK, M_DV), F32), jnp.zeros((1, M_DQK), F32), jnp.zeros((1, LANES), F32))
    states = [zero_state, zero_state]

    for j in range(n_ctx_chunks):
        for direction, jj in ((0, j), (1, n_ctx_chunks - 1 - j)):
            _, states[direction] = _mlstm_chunk(
                direction, head, None, kc_ref[jj * L:(jj + 1) * L, :], ktc_ref[0, 0, jj],
                vc_ref[jj * L:(jj + 1) * L, :], None, grow_ref[0, 0, jj],
                states[direction], valid[direction])

    def body(j, carry):
        carry = list(carry)
        for direction, jj, h_ref in ((0, j, hf_ref), (1, n_lat_chunks - 1 - j, hb_ref)):
            r0 = pl.multiple_of(jj * L, L)
            h, carry[direction] = _mlstm_chunk(
                direction, head, q_ref[pl.ds(r0, L), :], k_ref[pl.ds(r0, L), :], kt_ref[0, 0, jj],
                v_ref[pl.ds(r0, L), :], gtok_ref[pl.ds(r0, L), :], grow_ref[0, 0, n_ctx_chunks + jj],
                carry[direction], valid[direction])
            h_ref[pl.ds(r0, L), :] = h
        return tuple(carry)

    lax.fori_loop(0, n_lat_chunks, body, tuple(states))

    def finish(j, carry):
        r0 = pl.multiple_of(j * L, L)
        hh = hf_ref[pl.ds(r0, L), :] + hb_ref[pl.ds(r0, L), :]
        hh = hh * lax.rsqrt(jnp.mean(hh * hh, axis=-1, keepdims=True) + EPS)
        og = jax.nn.sigmoid(o_ref[pl.ds(r0, L), :].astype(F32))
        out_ref[pl.ds(r0, L), :] = (hh * ng_ref[...] * og).astype(out_ref.dtype)
        return carry

    lax.fori_loop(0, n_lat_chunks, finish, 0)


def _chunked_kt(k2d, b, n_chunks):
    k = k2d.reshape(b, n_chunks, M_CHUNK, M_HEADS, M_DQK)
    return jnp.transpose(k, (0, 3, 1, 4, 2))


def _mlstm(p_all, b, s, n_ctx, g_all, grow, norm_g):
    L = M_CHUNK
    nck, nlk = n_ctx // L, s // L
    t_lat = b * s
    cb = t_lat // n_ctx
    kt_lat = _chunked_kt(p_all[:t_lat, P_KM:P_KM + M_QK_W], b, nlk)
    kt_ctx = _chunked_kt(p_all[t_lat:, P_KM:P_KM + M_QK_W], b, nck)
    kern = functools.partial(_mlstm_kernel, n_ctx_chunks=nck, n_lat_chunks=nlk)
    return pl.pallas_call(
        kern,
        grid=(b, M_HEADS),
        in_specs=[pl.BlockSpec((s, M_DQK), lambda bi, h: (bi, P_QM // M_DQK + h)),
                  pl.BlockSpec((s, M_DQK), lambda bi, h: (bi, P_KM // M_DQK + h)),
                  pl.BlockSpec((1, 1, nlk, M_DQK, L), lambda bi, h: (bi, h, 0, 0, 0)),
                  pl.BlockSpec((s, M_DV), lambda bi, h: (bi, P_VM // M_DV + h)),
                  pl.BlockSpec((s, M_DV), lambda bi, h: (bi, P_OM // M_DV + h)),
                  pl.BlockSpec((n_ctx, M_DQK), lambda bi, h: (cb + bi, P_KM // M_DQK + h)),
                  pl.BlockSpec((1, 1, nck, M_DQK, L), lambda bi, h: (bi, h, 0, 0, 0)),
                  pl.BlockSpec((n_ctx, M_DV), lambda bi, h: (cb + bi, P_VM // M_DV + h)),
                  pl.BlockSpec((s, LANES), lambda bi, h: (bi, 0)),
                  pl.BlockSpec((1, 1, nck + nlk, 4, L), lambda bi, h: (bi, h, 0, 0, 0)),
                  pl.BlockSpec((1, M_DV), lambda bi, h: (0, h))],
        out_specs=pl.BlockSpec((s, M_DV), lambda bi, h: (bi, h)),
        out_shape=jax.ShapeDtypeStruct((t_lat, M_V_W), BF16),
        scratch_shapes=[pltpu.VMEM((s, M_DV), F32),
                        pltpu.VMEM((s, M_DV), F32)],
        compiler_params=_cparams(("parallel", "parallel")),
        name="mlstm",
    )(p_all, p_all, kt_lat, p_all, p_all, p_all, kt_ctx, p_all, g_all, grow, norm_g.reshape(1, M_V_W))


def _rope(t, cos_t, sin_s, lane_lo):
    swapped = jnp.where(lane_lo, pltpu.roll(t, A_DH - ROPE_HALF, 1), pltpu.roll(t, ROPE_HALF, 1))
    return t * cos_t + swapped * sin_s


def _attn_kernel(sink_ref, q_ref, k_ref, v_ref, kc_ref, vc_ref, cos_ref, sin_ref, out_ref, kpad, vpad):
    s = k_ref.shape[0]
    blk = A_BLOCK
    hk = pl.program_id(1)
    log2e = 1.4426950408889634
    qscale = (A_DH ** -0.5) * log2e
    lane = lax.broadcasted_iota(jnp.int32, (1, A_DH), 1)
    lane_lo = (lane % (2 * ROPE_HALF)) < ROPE_HALF

    zero = jnp.zeros((blk, A_DH), BF16)
    kpad[0:blk, :] = zero
    kpad[s + blk:s + 2 * blk, :] = zero
    vpad[0:blk, :] = zero
    vpad[s + blk:s + 2 * blk, :] = zero
    vpad[blk:s + blk, :] = v_ref[...]

    def rope_k(n, carry):
        r0 = pl.multiple_of(n * blk, blk)
        kf = k_ref[pl.ds(r0, blk), :].astype(F32)
        kr = _rope(kf, cos_ref[pl.ds(r0, blk), :], sin_ref[pl.ds(r0, blk), :], lane_lo)
        kpad[pl.ds(r0 + blk, blk), :] = kr.astype(BF16)
        return carry

    lax.fori_loop(0, s // blk, rope_k, 0)

    kc = kc_ref[...]
    n_ctx = kc.shape[0]
    vc = vc_ref[...]
    n_blocks = s // blk
    rr = lax.broadcasted_iota(jnp.int32, (blk, blk), 0)
    cc = lax.broadcasted_iota(jnp.int32, (blk, blk), 1)
    nt = (((1,), (1,)), ((), ()))
    neg = -jnp.inf

    def q_block(n):
        r0 = pl.multiple_of(n * blk, blk)
        cos_q = cos_ref[pl.ds(r0, blk), :] * qscale
        sin_q = sin_ref[pl.ds(r0, blk), :] * qscale
        kb = kpad[pl.ds(r0, 3 * blk), :]
        vb = vpad[pl.ds(r0, 3 * blk), :]
        bias_prev = jnp.where(cc >= rr, jnp.where(n > 0, 0.0, neg), neg)
        bias_next = jnp.where(cc <= rr, jnp.where(n < n_blocks - 1, 0.0, neg), neg)
        q_all = jnp.concatenate(
            [_rope(q_ref[pl.ds(r0, blk), g * A_DH:(g + 1) * A_DH].astype(F32), cos_q, sin_q,
                   lane_lo).astype(BF16) for g in range(A_GROUP)], axis=0)
        s_loc = lax.dot_general(q_all, kb, nt, preferred_element_type=F32)
        s_ctx = lax.dot_general(q_all, kc, nt, preferred_element_type=F32)
        p_parts, inv_den = [], []
        for g in range(A_GROUP):
            rows = slice(g * blk, (g + 1) * blk)
            tiles = ([s_loc[rows, 0:blk] + bias_prev, s_loc[rows, blk:2 * blk],
                      s_loc[rows, 2 * blk:3 * blk] + bias_next]
                     + [s_ctx[rows, c:c + blk] for c in range(0, n_ctx, blk)])
            tmax = functools.reduce(jnp.maximum, tiles)
            snk2 = sink_ref[hk * A_GROUP + g] * log2e
            m2 = jnp.maximum(jnp.max(tmax, axis=1, keepdims=True), snk2)
            p = [jnp.exp2(t - m2) for t in tiles]
            den = jnp.sum(functools.reduce(jnp.add, p), axis=1, keepdims=True) + jnp.exp2(snk2 - m2)
            inv_den.append(1.0 / den)
            p_parts.append(jnp.concatenate([t.astype(BF16) for t in p], axis=1))
        p_all = jnp.concatenate(p_parts, axis=0)
        o_all = (jnp.dot(p_all[:, :3 * blk], vb, preferred_element_type=F32)
                 + jnp.dot(p_all[:, 3 * blk:], vc, preferred_element_type=F32))
        for g in range(A_GROUP):
            o = o_all[g * blk:(g + 1) * blk, :] * inv_den[g]
            out_ref[pl.ds(r0, blk), g * A_DH:(g + 1) * A_DH] = o.astype(out_ref.dtype)

    def body(i, carry):
        for u in range(ATTN_UNROLL):
            q_block(i * ATTN_UNROLL + u)
        return carry

    lax.fori_loop(0, n_blocks // ATTN_UNROLL, body, 0)


def _attention(p_all, b, s, n_ctx, sink, cos_t, sin_s):
    gw = A_GROUP * A_DH
    cb = (b * s) // n_ctx
    return pl.pallas_call(
        _attn_kernel,
        grid=(b, A_KV_HEADS),
        in_specs=[pl.BlockSpec(memory_space=pltpu.SMEM),
                  pl.BlockSpec((s, gw), lambda bi, h: (bi, P_QA // gw + h)),
                  pl.BlockSpec((s, A_DH), lambda bi, h: (bi, P_KA // A_DH + h)),
                  pl.BlockSpec((s, A_DH), lambda bi, h: (bi, P_VA // A_DH + h)),
                  pl.BlockSpec((n_ctx, A_DH), lambda bi, h: (cb + bi, P_KA // A_DH + h)),
                  pl.BlockSpec((n_ctx, A_DH), lambda bi, h: (cb + bi, P_VA // A_DH + h)),
                  pl.BlockSpec((s, A_DH), lambda bi, h: (0, 0)),
                  pl.BlockSpec((s, A_DH), lambda bi, h: (0, 0))],
        out_specs=pl.BlockSpec((s, gw), lambda bi, h: (bi, h)),
        out_shape=jax.ShapeDtypeStruct((b * s, A_Q_W), BF16),
        scratch_shapes=[pltpu.VMEM((s + 2 * A_BLOCK, A_DH), BF16),
                        pltpu.VMEM((s + 2 * A_BLOCK, A_DH), BF16)],
        compiler_params=_cparams(("parallel", "parallel")),
        name="attention",
    )(sink, p_all, p_all, p_all, p_all, p_all, cos_t, sin_s)


def _rope_tables(n_tokens):
    rows = n_tokens // GRID_W
    row = jnp.repeat(jnp.arange(rows, dtype=F32), GRID_W)
    col = jnp.tile(jnp.arange(GRID_W, dtype=F32), rows)
    inv_freq = ROPE_BASE ** (-jnp.arange(ROPE_HALF, dtype=F32) / ROPE_HALF)
    ar = row[:, None] * inv_freq
    ac = col[:, None] * inv_freq
    cos_t = jnp.concatenate([jnp.cos(ar), jnp.cos(ar), jnp.cos(ac), jnp.cos(ac)], axis=1)
    sin_s = jnp.concatenate([-jnp.sin(ar), jnp.sin(ar), -jnp.sin(ac), jnp.sin(ac)], axis=1)
    return cos_t, sin_s


def _merge_kernel(hm_ref, at_ref, wm_ref, wa_ref, gm_ref, ga_ref, o_ref):
    bm = jnp.dot(hm_ref[...], wm_ref[...], preferred_element_type=F32)
    ba = jnp.dot(at_ref[...], wa_ref[...], preferred_element_type=F32)
    o = jax.nn.sigmoid(gm_ref[...].astype(F32)) * bm + jax.nn.sigmoid(ga_ref[...].astype(F32)) * ba
    o_ref[...] = o.astype(o_ref.dtype)


def _merge(hm, att, p2d, w_br_m, w_br_a):
    t, d = hm.shape
    tm, tn = 1024, 1024
    assert P_GBM % tn == 0 and P_GBA % tn == 0
    return pl.pallas_call(
        _merge_kernel,
        grid=(t // tm, d // tn),
        in_specs=[pl.BlockSpec((tm, M_V_W), lambda i, j: (i, 0)),
                  pl.BlockSpec((tm, A_Q_W), lambda i, j: (i, 0)),
                  pl.BlockSpec((M_V_W, tn), lambda i, j: (0, j)),
                  pl.BlockSpec((A_Q_W, tn), lambda i, j: (0, j)),
                  pl.BlockSpec((tm, tn), lambda i, j: (i, P_GBM // tn + j)),
                  pl.BlockSpec((tm, tn), lambda i, j: (i, P_GBA // tn + j))],
        out_specs=pl.BlockSpec((tm, tn), lambda i, j: (i, j)),
        out_shape=jax.ShapeDtypeStruct((t, d), BF16),
        compiler_params=_cparams(("parallel", "arbitrary")),
        name="merge",
    )(hm, att, w_br_m, w_br_a, p2d, p2d)


def _outproj_kernel(mg_ref, x_ref, wo_ref, g1_ref, ng_ref, sh_ref, sc_ref, wr_ref, br_ref,
                    x1_ref, h2_ref, ri_ref, rw_ref):
    tm = mg_ref.shape[1]
    for r in range(0, tm, tm // OUTPROJ_SPLIT):
        rows = slice(r, r + tm // OUTPROJ_SPLIT)
        y = jnp.dot(mg_ref[0, rows, :], wo_ref[...], preferred_element_type=F32)
        x1 = x_ref[0, rows, :] + g1_ref[0] * y
        x1_ref[0, rows, :] = x1
        hn = x1 * lax.rsqrt(jnp.mean(x1 * x1, axis=-1, keepdims=True) + EPS) * ng_ref[...]
        h2 = hn * (1.0 + sc_ref[0]) + sh_ref[0]
        h2_ref[0, rows, :] = h2

        logits = jnp.dot(h2.astype(BF16), wr_ref[...], preferred_element_type=F32) + br_ref[...]
        lane = lax.broadcasted_iota(jnp.int32, logits.shape, 1)
        neg = -jnp.inf
        gl = jnp.where(lane < N_GROUPS, logits, neg)
        gmax = jnp.max(gl, axis=1, keepdims=True)
        gsel = jnp.min(jnp.where(gl == gmax, lane, LANES), axis=1, keepdims=True)
        p_sel = 1.0 / jnp.sum(jnp.exp(gl - gmax), axis=1, keepdims=True)
        lo = N_GROUPS + gsel * EXPERTS_PER_GROUP
        el = jnp.where((lane >= lo) & (lane < lo + EXPERTS_PER_GROUP), logits, neg)
        v1 = jnp.max(el, axis=1, keepdims=True)
        i1 = jnp.min(jnp.where(el == v1, lane, LANES), axis=1, keepdims=True)
        el2 = jnp.where(lane == i1, neg, el)
        v2 = jnp.max(el2, axis=1, keepdims=True)
        i2 = jnp.min(jnp.where(el2 == v2, lane, LANES), axis=1, keepdims=True)
        e2 = jnp.exp(v2 - v1)
        w1 = p_sel / (1.0 + e2)
        w2 = p_sel * e2 / (1.0 + e2)
        ri_ref[0, rows, :] = jnp.where(lane == 0, i1 - N_GROUPS, jnp.where(lane == 1, i2 - N_GROUPS, 0))
        rw_ref[0, rows, :] = jnp.where(lane == 0, w1, jnp.where(lane == 1, w2, 0.0))


def _outproj(merged, x, w_out, g1, norm_g, shift, scale, w_r, b_r):
    b, s, d = x.shape
    tm = 512
    row = lambda bi, i: (bi, i, 0)
    per_b = lambda bi, i: (bi, 0, 0)
    fixed = lambda bi, i: (0, 0)
    return pl.pallas_call(
        _outproj_kernel,
        grid=(b, s // tm),
        in_specs=[pl.BlockSpec((1, tm, d), row),
                  pl.BlockSpec((1, tm, d), row),
                  pl.BlockSpec((d, d), fixed),
                  pl.BlockSpec((1, 1, d), per_b),
                  pl.BlockSpec((1, d), fixed),
                  pl.BlockSpec((1, 1, d), per_b),
                  pl.BlockSpec((1, 1, d), per_b),
                  pl.BlockSpec((d, LANES), fixed),
                  pl.BlockSpec((1, LANES), fixed)],
        out_specs=[pl.BlockSpec((1, tm, d), row),
                   pl.BlockSpec((1, tm, d), row),
                   pl.BlockSpec((1, tm, LANES), row),
                   pl.BlockSpec((1, tm, LANES), row)],
        out_shape=[jax.ShapeDtypeStruct((b, s, d), F32),
                   jax.ShapeDtypeStruct((b, s, d), F32),
                   jax.ShapeDtypeStruct((b, s, LANES), jnp.int32),
                   jax.ShapeDtypeStruct((b, s, LANES), F32)],
        compiler_params=_cparams(("parallel", "parallel")),
        name="outproj_router",
    )(merged, x, w_out, g1, norm_g.reshape(1, d), shift, scale, w_r, b_r)


def _moe_kernel(sbe_ref, nsub_ref, base_ref, nused_ref, tok_ref, h_hbm, wg_hbm, wu_hbm, wd_hbm, y_hbm,
                xbuf, xb, ring_gu, ring_d, wb_gu, wb_d, hgu, act, ostage, sem, wsem, osem, *, n_super):
    nused = nused_ref[0]
    d = h_hbm.shape[1]
    kh = d // 2
    fh = D_EXPERT // 2

    def w_copy(c, sb):
        e = sbe_ref[sb]
        if c < 4:
            w_hbm = wg_hbm if c % 2 == 0 else wu_hbm
            return pltpu.make_async_copy(w_hbm.at[0, e, pl.ds((c // 2) * kh, kh), :], ring_gu.at[c],
                                         wsem.at[c])
        return pltpu.make_async_copy(wd_hbm.at[0, e, pl.ds((c - 4) * fh, fh), :], ring_d.at[c - 4],
                                     wsem.at[c])

    def out_copy(sb, p):
        return pltpu.make_async_copy(ostage.at[p], y_hbm.at[pl.ds(sb * MOE_SUPER, MOE_SUPER)], osem.at[p])

    def start_gather(sb, slot):
        base = base_ref[sb]

        def issue(i, carry):
            for u in range(GATHER_UNROLL):
                r = i * GATHER_UNROLL + u
                tok = tok_ref[base + r]
                pltpu.make_async_copy(h_hbm.at[pl.ds(tok, 1)], xbuf.at[slot, pl.ds(r, 1)],
                                      sem.at[slot]).start()
            return carry

        lax.fori_loop(0, nsub_ref[sb] * (MOE_SUB // GATHER_UNROLL), issue, 0)

    def wait_gather(sb, slot):
        def wait_sub(i, carry):
            pltpu.make_async_copy(h_hbm.at[pl.ds(0, MOE_SUB)], xbuf.at[slot, pl.ds(0, MOE_SUB)],
                                  sem.at[slot]).wait()
            return carry

        lax.fori_loop(0, nsub_ref[sb], wait_sub, 0)

    def job_compute(c, n, p):
        def run(m):
            rows = slice(0, m * MOE_SUB)
            if c < 4:
                part = jnp.dot(xb[rows, (c // 2) * kh:(c // 2 + 1) * kh], wb_gu[...],
                               preferred_element_type=F32)
                if c < 2:
                    hgu[c, rows, :] = part
                elif c == 2:
                    hgu[0, rows, :] += part
                else:
                    g = hgu[0, rows, :]
                    act[rows, :] = (g * jax.nn.sigmoid(g) * (hgu[1, rows, :] + part)).astype(BF16)
            else:
                part = jnp.dot(act[rows, (c - 4) * fh:(c - 3) * fh], wb_d[...],
                               preferred_element_type=F32)
                if c == 4:
                    hgu[0, rows, :] = part[:, :kh]
                    hgu[1, rows, :] = part[:, kh:]
                else:
                    y = part + jnp.concatenate([hgu[0, rows, :], hgu[1, rows, :]], axis=1)
                    ostage[p, rows, :] = y.astype(ostage.dtype)

        lax.switch(n - 1, [functools.partial(run, m) for m in range(1, MOE_SUPER // MOE_SUB + 1)])

    ostage[...] = jnp.zeros_like(ostage)
    start_gather(0, 0)
    for c in range(6):
        w_copy(c, 0).start()

    def superblock(s, carry):
        slot = s % 2
        n = nsub_ref[s]
        wait_gather(s, slot)

        @pl.when(s + 1 < nused)
        def _():
            start_gather(s + 1, 1 - slot)

        def cast_x(i, c2):
            rows = pl.ds(pl.multiple_of(i * MOE_SUB, MOE_SUB), MOE_SUB)
            xb[rows, :] = xbuf[slot, rows, :].astype(BF16)
            return c2

        lax.fori_loop(0, n, cast_x, 0)

        for c in range(6):
            w_copy(c, s).wait()
            if c < 4:
                wb_gu[...] = ring_gu[c].astype(BF16)
            else:
                wb_d[...] = ring_d[c - 4].astype(BF16)

            @pl.when(s + 1 < nused)
            def _():
                w_copy(c, s + 1).start()

            if c == 5:
                @pl.when(s >= 2)
                def _():
                    out_copy(s - 2, slot).wait()

            job_compute(c, n, slot)

        out_copy(s, slot).start()
        return carry

    lax.fori_loop(0, nused, superblock, 0)

    @pl.when(nused >= 2)
    def _():
        out_copy(nused - 2, nused % 2).wait()

    out_copy(nused - 1, (nused - 1) % 2).wait()

    ostage[0] = jnp.zeros(ostage.shape[1:], ostage.dtype)

    def zero_start(sb, carry):
        out_copy(sb, 0).start()
        return carry

    def zero_wait(sb, carry):
        out_copy(sb, 0).wait()
        return carry

    lax.fori_loop(nused, n_super, zero_start, 0)
    lax.fori_loop(nused, n_super, zero_wait, 0)


def _moe_experts(h2, sorted_tok, sb_expert, sb_nsub, sb_base, n_used, n_super, w_g, w_u, w_d):
    _, d = h2.shape
    assert d % 2 == 0 and D_EXPERT % 2 == 0 and d // 2 == D_EXPERT
    any_spec = pl.BlockSpec(memory_space=pl.ANY)
    grid_spec = pltpu.PrefetchScalarGridSpec(
        num_scalar_prefetch=5,
        grid=(1,),
        in_specs=[any_spec, any_spec, any_spec, any_spec],
        out_specs=any_spec,
        scratch_shapes=[pltpu.VMEM((2, MOE_SUPER, d), F32),
                        pltpu.VMEM((MOE_SUPER, d), BF16),
                        pltpu.VMEM((4, d // 2, D_EXPERT), F32),
                        pltpu.VMEM((2, D_EXPERT // 2, d), F32),
                        pltpu.VMEM((d // 2, D_EXPERT), BF16),
                        pltpu.VMEM((D_EXPERT // 2, d), BF16),
                        pltpu.VMEM((2, MOE_SUPER, D_EXPERT), F32),
                        pltpu.VMEM((MOE_SUPER, D_EXPERT), BF16),
                        pltpu.VMEM((2, MOE_SUPER, d), BF16),
                        pltpu.SemaphoreType.DMA((2,)),
                        pltpu.SemaphoreType.DMA((6,)),
                        pltpu.SemaphoreType.DMA((2,))],
    )
    return pl.pallas_call(
        functools.partial(_moe_kernel, n_super=n_super),
        grid_spec=grid_spec,
        out_shape=jax.ShapeDtypeStruct((n_super * MOE_SUPER, d), BF16),
        compiler_params=_cparams(("arbitrary",), vmem=MOE_VMEM_LIMIT),
        name="moe_experts",
    )(sb_expert, sb_nsub, sb_base, n_used, sorted_tok, h2, w_g, w_u, w_d)


def _dispatch_tables(eid, n_super):
    flat = eid.reshape(-1)
    n_assign = flat.shape[0]
    order = jnp.argsort(flat).astype(jnp.int32)
    sorted_tok = jnp.concatenate([order // TOP_K, jnp.zeros((MOE_SUB,), jnp.int32)])
    counts = jnp.bincount(flat, length=N_EXPERTS)
    nsup = (counts + MOE_SUPER - 1) // MOE_SUPER
    sup_end = jnp.cumsum(nsup)
    sup_start = sup_end - nsup
    start = jnp.cumsum(counts) - counts
    off = sup_start * MOE_SUPER - start
    d_off = jnp.diff(off, prepend=0)
    pos = jnp.arange(n_assign)
    off_sorted = jnp.sum(jnp.where(pos[:, None] >= start[None, :], d_off[None, :], 0), axis=1)
    dest_sorted = (pos + off_sorted).astype(jnp.int32)
    _, dest = lax.sort((order, dest_sorted), num_keys=1)
    sidx = jnp.arange(n_super)
    sb_expert = jnp.minimum(jnp.sum(sup_end[None, :] <= sidx[:, None], axis=1), N_EXPERTS - 1)
    n_used = sup_end[-1]
    k = sidx - sup_start[sb_expert]
    rows_in = jnp.clip(counts[sb_expert] - k * MOE_SUPER, 0, MOE_SUPER)
    sb_nsub = jnp.where(sidx < n_used, (rows_in + MOE_SUB - 1) // MOE_SUB, 0)
    sb_base = start[sb_expert] + k * MOE_SUPER
    i32 = lambda a: a.astype(jnp.int32)
    return (sorted_tok, dest.reshape(-1, TOP_K), i32(sb_expert), i32(sb_nsub), i32(sb_base),
            i32(n_used).reshape(1))


def _final_kernel(x1_ref, ya_ref, yb_ref, w_ref, g2_ref, fg_ref, o_ref):
    w = w_ref[0]
    moe = w[:, 0:1] * ya_ref[0].astype(F32) + w[:, 1:2] * yb_ref[0].astype(F32)
    x2 = x1_ref[0] + g2_ref[0] * moe
    o_ref[0] = x2 * lax.rsqrt(jnp.mean(x2 * x2, axis=-1, keepdims=True) + EPS) * fg_ref[...]


def _final(x1, ya, yb, w_top, g2, final_g):
    b, s, d = x1.shape
    ts = 512
    row = lambda bi, i: (bi, i, 0)
    return pl.pallas_call(
        _final_kernel,
        grid=(b, s // ts),
        in_specs=[pl.BlockSpec((1, ts, d), row),
                  pl.BlockSpec((1, ts, d), row),
                  pl.BlockSpec((1, ts, d), row),
                  pl.BlockSpec((1, ts, LANES), row),
                  pl.BlockSpec((1, 1, d), lambda bi, i: (bi, 0, 0)),
                  pl.BlockSpec((1, d), lambda bi, i: (0, 0))],
        out_specs=pl.BlockSpec((1, ts, d), row),
        out_shape=jax.ShapeDtypeStruct((b, s, d), F32),
        compiler_params=_cparams(("parallel", "parallel")),
        name="final_norm",
    )(x1, ya, yb, w_top, g2, final_g.reshape(1, d))


def kernel(x, c, ctx, c_ctx, w_mod, b_mod, norm1_g, w_in, mlstm_gate_b, mlstm_norm_g, attn_sink, w_br_m, w_br_a, w_out, norm2_g, w_router_grp, b_router_grp, w_router_exp, b_router_exp, w_exp_gate, w_exp_up, w_exp_down, final_norm_g):
    b, s, d = x.shape
    n_ctx = ctx.shape[1]
    assert w_mod.shape[0] == 1, "single-layer block"

    pad_rows = 8 - (b + 1)
    cvec = jnp.concatenate([c, c_ctx[None, :], jnp.zeros((pad_rows, d), F32)], axis=0)
    mods = _adaln(cvec, w_mod[0], b_mod[0])
    g1, sh2, sc2, g2 = [mods[:b, i * d:(i + 1) * d].reshape(b, 1, d) for i in range(2, 6)]
    sh1_all = mods[:, 0:d].reshape(-1, 1, d)
    sc1_all = mods[:, d:2 * d].reshape(-1, 1, d)

    t_lat = b * s
    w_in_t = jnp.transpose(w_in[0])
    b_gate = jnp.pad(mlstm_gate_b[0].astype(F32).reshape(1, GATE_W), ((0, 0), (0, LANES - GATE_W)))
    h_all, g_all = _norm_mod(x.reshape(t_lat, d), ctx.reshape(b * n_ctx, d), norm1_g[0], sh1_all, sc1_all,
                             s, b, w_in_t, b_gate)
    p_all = _proj(h_all, w_in_t, t_lat)

    g_lat = g_all[:t_lat, :GATE_W].reshape(b, s // M_CHUNK, M_CHUNK, 4, M_HEADS)
    g_ctx = g_all[t_lat:, :GATE_W].reshape(b, n_ctx // M_CHUNK, M_CHUNK, 4, M_HEADS)
    grow = jnp.transpose(jnp.concatenate([g_ctx, g_lat], axis=1), (0, 4, 1, 3, 2))

    hm = _mlstm(p_all, b, s, n_ctx, g_all, grow, mlstm_norm_g[0])
    cos_t, sin_s = _rope_tables(s)
    att = _attention(p_all, b, s, n_ctx, attn_sink[0], cos_t, sin_s)

    merged = _merge(hm, att, p_all, w_br_m[0].astype(BF16), w_br_a[0].astype(BF16))

    w_r = jnp.concatenate([w_router_grp[0], w_router_exp[0]], axis=1)
    n_r = w_r.shape[1]
    w_r = jnp.pad(w_r, ((0, 0), (0, LANES - n_r))).astype(BF16)
    b_r = jnp.pad(jnp.concatenate([b_router_grp[0], b_router_exp[0]]), (0, LANES - n_r)).reshape(1, LANES)
    x1, h2, r_idx, r_w = _outproj(merged.reshape(b, s, d), x, w_out[0].astype(BF16), g1, norm2_g[0],
                                  sh2, sc2, w_r, b_r)

    t_tok = b * s
    eid = r_idx.reshape(t_tok, LANES)[:, :TOP_K]
    n_super = (t_tok * TOP_K) // MOE_SUPER + N_EXPERTS
    sorted_tok, dest, sb_expert, sb_nsub, sb_base, n_used = _dispatch_tables(eid, n_super)
    ybuf = _moe_experts(h2.reshape(t_tok, d), sorted_tok, sb_expert, sb_nsub, sb_base, n_used, n_super,
                        w_exp_gate, w_exp_up, w_exp_down)
    ya = ybuf.at[dest[:, 0]].get(mode="promise_in_bounds").reshape(b, s, d)
    yb = ybuf.at[dest[:, 1]].get(mode="promise_in_bounds").reshape(b, s, d)

    return _final(x1, ya, yb, r_w, g2, final_norm_g)
```
